```python
import jax, jax.numpy as jnp
from jax import lax
import numpy as np

D_MODEL = 1024
BATCH = 32
SEQ = 2048
DEPTH = 2

D_MIX = D_MODEL
D_RNN = D_MIX // 2
D_RET = D_MIX - D_RNN
N_RNN_BLOCKS = 8
RNN_BLOCK = D_RNN // N_RNN_BLOCKS
CONV_WIDTH = 4
LRU_C = 8.0
N_RET_HEADS = 4
RET_HEAD_DIM = D_RET // N_RET_HEADS
CHUNK = 128
ROPE_BASE = 10000.0
D_FF = 2816
EPS = 1e-6
IN_SPLITS = [D_RNN, D_RNN, D_RET, D_RET, D_RET, D_RET]
D_IN = sum(IN_SPLITS)

kernel_name = 'hymba_rglru_retention_macaron'


def rmsnorm(x, g):
    xf = x.astype(jnp.float32)
    y = xf * lax.rsqrt(jnp.mean(xf * xf, axis=-1, keepdims=True) + EPS)
    return (y * g.astype(jnp.float32)).astype(x.dtype)


def head_rmsnorm(y):
    yf = y.astype(jnp.float32)
    return yf * lax.rsqrt(jnp.mean(yf * yf, axis=-1, keepdims=True) + EPS)


def swiglu(x, w_in, w_out):
    gate, up = jnp.split(x @ w_in, 2, axis=-1)
    return (jax.nn.silu(gate) * up) @ w_out


def causal_dwconv(x, w, b):
    seq = x.shape[1]
    k_taps = w.shape[0]
    xp = jnp.pad(x, ((0, 0), (k_taps - 1, 0), (0, 0)))
    y = xp[:, 0:seq] * w[0]
    for k in range(1, k_taps):
        y = y + xp[:, k:k + seq] * w[k]
    return y + b


def rg_lru(x, gate_a_w, gate_a_b, gate_x_w, gate_x_b, lam):
    bsz, seq, ch = x.shape
    xh = x.reshape(bsz, seq, N_RNN_BLOCKS, RNN_BLOCK)
    r_gate = jax.nn.sigmoid(jnp.einsum('bshi,hij->bshj', xh, gate_a_w).reshape(bsz, seq, ch) + gate_a_b).astype(jnp.float32)
    i_gate = jax.nn.sigmoid(jnp.einsum('bshi,hij->bshj', xh, gate_x_w).reshape(bsz, seq, ch) + gate_x_b).astype(jnp.float32)
    log_a = -LRU_C * r_gate * jax.nn.softplus(-lam.astype(jnp.float32))
    a = jnp.exp(log_a)
    u = jnp.sqrt(-jnp.expm1(2.0 * log_a)) * (i_gate * x.astype(jnp.float32))

    def combine(left, right):
        a1, b1 = left
        a2, b2 = right
        return a1 * a2, a2 * b1 + b2

    _, h = lax.associative_scan(combine, (a, u), axis=1)
    return h.astype(x.dtype)


def rotary(t, positions):
    half = t.shape[-1] // 2
    inv_freq = ROPE_BASE ** (-jnp.arange(half, dtype=jnp.float32) / half)
    ang = positions.astype(jnp.float32)[..., None] * inv_freq
    cos = jnp.cos(ang)[:, :, None, :]
    sin = jnp.sin(ang)[:, :, None, :]
    t1 = t[..., :half].astype(jnp.float32)
    t2 = t[..., half:].astype(jnp.float32)
    return jnp.concatenate([t1 * cos - t2 * sin, t1 * sin + t2 * cos], axis=-1).astype(t.dtype)


def chunkwise_retention(q, k, v, positions):
    bsz, seq, n_heads, dh = q.shape
    n_chunks = seq // CHUNK
    q = rotary(q, positions)
    k = rotary(k, positions) * (dh ** -0.5)
    log_gamma = jnp.log1p(-(2.0 ** (-5.0 - jnp.arange(n_heads, dtype=jnp.float32))))
    idx = jnp.arange(CHUNK, dtype=jnp.float32)
    rel = idx[:, None] - idx[None, :]
    decay = jnp.where(rel >= 0, jnp.exp(jnp.maximum(rel, 0.0)[None] * log_gamma[:, None, None]), 0.0)
    xi = jnp.exp((idx + 1.0)[None, :] * log_gamma[:, None])
    zeta = jnp.exp((CHUNK - 1.0 - idx)[None, :] * log_gamma[:, None])
    chunk_decay = jnp.exp(CHUNK * log_gamma)

    qc = q.reshape(bsz, n_chunks, CHUNK, n_heads, dh)
    kc = k.reshape(bsz, n_chunks, CHUNK, n_heads, dh)
    vc = v.reshape(bsz, n_chunks, CHUNK, n_heads, dh)

    scores = jnp.einsum('bnchd,bnmhd->bnhcm', qc, kc) * decay
    inner = jnp.einsum('bnhcm,bnmhe->bnche', scores, vc)
    kv = jnp.einsum('bnmhd,bnmhe,hm->nbhde', kc, vc, zeta)

    def step(state, kv_i):
        return chunk_decay[None, :, None, None] * state + kv_i, state

    _, prev = lax.scan(step, jnp.zeros(kv.shape[1:], kv.dtype), kv)
    cross = jnp.einsum('bnchd,nbhde->bnche', qc, prev) * xi.T[None, None, :, :, None]
    return (inner + cross).reshape(bsz, seq, n_heads, dh)


def setup_inputs(seed: int = 0) -> dict:
    key = jax.random.key(seed)
    ks = jax.random.split(key, 24)
    f32 = jnp.float32
    L = DEPTH

    def nrm(k, shape, scale):
        return jax.random.normal(k, shape, f32) * scale

    u = jax.random.uniform(ks[12], (L, D_RNN), f32, minval=0.9, maxval=0.999)
    a_base = u ** (1.0 / LRU_C)
    lru_lambda = jnp.log(a_base) - jnp.log1p(-a_base)
    offsets = jax.random.randint(ks[20], (BATCH, 1), 0, 4096, dtype=jnp.int32)
    positions = offsets + jnp.arange(SEQ, dtype=jnp.int32)[None, :]
    return {
        'x': nrm(ks[0], (BATCH, SEQ, D_MODEL), 1.0),
        'positions': positions,
        'norm_ffn1': 1.0 + nrm(ks[1], (L, D_MODEL), 0.02),
        'w_ffn1_in': nrm(ks[2], (L, D_MODEL, 2 * D_FF), D_MODEL ** -0.5),
        'w_ffn1_out': nrm(ks[3], (L, D_FF, D_MODEL), D_FF ** -0.5),
        'norm_mix': 1.0 + nrm(ks[4], (L, D_MODEL), 0.02),
        'w_in': nrm(ks[5], (L, D_MODEL, D_IN), D_MODEL ** -0.5),
        'conv_w': nrm(ks[6], (L, CONV_WIDTH, D_RNN), CONV_WIDTH ** -0.5),
        'conv_b': nrm(ks[7], (L, D_RNN), 0.01),
        'gate_a_w': nrm(ks[8], (L, N_RNN_BLOCKS, RNN_BLOCK, RNN_BLOCK), RNN_BLOCK ** -0.5),
        'gate_a_b': nrm(ks[9], (L, D_RNN), 0.01),
        'gate_x_w': nrm(ks[10], (L, N_RNN_BLOCKS, RNN_BLOCK, RNN_BLOCK), RNN_BLOCK ** -0.5),
        'gate_x_b': nrm(ks[11], (L, D_RNN), 0.01),
        'lru_lambda': lru_lambda,
        'w_out': nrm(ks[13], (L, D_MIX, D_MODEL), D_MIX ** -0.5),
        'norm_ffn2': 1.0 + nrm(ks[14], (L, D_MODEL), 0.02),
        'w_ffn2_in': nrm(ks[15], (L, D_MODEL, 2 * D_FF), D_MODEL ** -0.5),
        'w_ffn2_out': nrm(ks[16], (L, D_FF, D_MODEL), D_FF ** -0.5),
        'norm_final': 1.0 + nrm(ks[17], (D_MODEL,), 0.02),
    }


def reference(x, positions, norm_ffn1, w_ffn1_in, w_ffn1_out, norm_mix, w_in, conv_w, conv_b,
              gate_a_w, gate_a_b, gate_x_w, gate_x_b, lru_lambda, w_out, norm_ffn2,
              w_ffn2_in, w_ffn2_out, norm_final):
    bsz, seq, _ = x.shape
    split_idx = [int(s) for s in np.cumsum(IN_SPLITS)[:-1]]
    for l in range(DEPTH):
        x = x + 0.5 * swiglu(rmsnorm(x, norm_ffn1[l]), w_ffn1_in[l], w_ffn1_out[l])

        h = rmsnorm(x, norm_mix[l])
        proj = h @ w_in[l]
        rnn_x, rnn_gate, q, k, v, g = jnp.split(proj, split_idx, axis=-1)

        rnn_x = causal_dwconv(rnn_x, conv_w[l], conv_b[l])
        rnn_y = rg_lru(rnn_x, gate_a_w[l], gate_a_b[l], gate_x_w[l], gate_x_b[l], lru_lambda[l])
        rnn_y = rnn_y * jax.nn.gelu(rnn_gate)

        heads = (bsz, seq, N_RET_HEADS, RET_HEAD_DIM)
        ret = chunkwise_retention(q.reshape(heads), k.reshape(heads), v.reshape(heads), positions)
        ret_y = head_rmsnorm(ret).reshape(bsz, seq, D_RET).astype(x.dtype) * jax.nn.silu(g)

        x = x + jnp.concatenate([rnn_y, ret_y], axis=-1) @ w_out[l]

        x = x + 0.5 * swiglu(rmsnorm(x, norm_ffn2[l]), w_ffn2_in[l], w_ffn2_out[l])
    return rmsnorm(x, norm_final)
```

```python
import functools

import jax
import jax.numpy as jnp
from jax import lax
from jax.experimental import pallas as pl
from jax.experimental.pallas import tpu as pltpu

F32 = jnp.float32
BF16 = jnp.bfloat16

D_MODEL = 1024
DEPTH = 2
D_RNN = 512
D_RET = 512
N_RNN_BLOCKS = 8
RNN_BLOCK = D_RNN // N_RNN_BLOCKS
CONV_WIDTH = 4
LRU_C = 8.0
N_RET_HEADS = 4
RET_HEAD_DIM = D_RET // N_RET_HEADS
CHUNK = 128
ROPE_BASE = 10000.0
D_FF = 2816
EPS = 1e-6
D_IN = 2 * D_RNN + 4 * D_RET

LANES = 128
SUBLANES = 8
MXU_DIM = 256
VMEM_LIMIT_BYTES = 60000 * 1024

FFN_ROWS = 512
FFN_COLS = MXU_DIM
MIX_NB = 8
MIX_T = CHUNK
HALO = SUBLANES
SCAN_PITCH = MIX_T + SUBLANES
ROPE_T = 256


def _rmsnorm(x, g):
    return x * lax.rsqrt(jnp.mean(x * x, axis=-1, keepdims=True) + EPS) * g


def _resident(shape):
    zeros = (0,) * len(shape)
    return pl.BlockSpec(shape, lambda *_: zeros, pipeline_mode=pl.Buffered(1))


def _ffn_kernel(x_ref, g_ref, w1_ref, w2_ref, gf_ref, o_ref, act_ref, *, final_norm):
    x = x_ref[...]
    xb = _rmsnorm(x, g_ref[...]).astype(BF16)
    for j in range(D_FF // FFN_COLS):
        lo = j * FFN_COLS
        gate = jnp.dot(xb, w1_ref[:, lo:lo + FFN_COLS], preferred_element_type=F32)
        up = jnp.dot(xb, w1_ref[:, D_FF + lo:D_FF + lo + FFN_COLS], preferred_element_type=F32)
        act_ref[:, lo:lo + FFN_COLS] = (gate * jax.nn.sigmoid(gate) * up).astype(BF16)
    y = jnp.dot(act_ref[...], w2_ref[...], preferred_element_type=F32)
    out = x + 0.5 * y
    if final_norm:
        out = _rmsnorm(out, gf_ref[...])
    o_ref[...] = out


def _ffn(x2d, g, w1, w2, gf, *, final_norm):
    rows = x2d.shape[0]
    assert rows % FFN_ROWS == 0 and D_FF % FFN_COLS == 0
    return pl.pallas_call(
        functools.partial(_ffn_kernel, final_norm=final_norm),
        out_shape=jax.ShapeDtypeStruct(x2d.shape, F32),
        grid=(rows // FFN_ROWS,),
        in_specs=[
            pl.BlockSpec((FFN_ROWS, D_MODEL), lambda i: (i, 0)),
            _resident((1, D_MODEL)),
            _resident((D_MODEL, 2 * D_FF)),
            _resident((D_FF, D_MODEL)),
            _resident((1, D_MODEL)),
        ],
        out_specs=pl.BlockSpec((FFN_ROWS, D_MODEL), lambda i: (i, 0)),
        scratch_shapes=[pltpu.VMEM((FFN_ROWS, D_FF), BF16)],
        compiler_params=pltpu.CompilerParams(
            dimension_semantics=("parallel",), vmem_limit_bytes=VMEM_LIMIT_BYTES),
        name="ffn_final" if final_norm else "ffn",
    )(x2d, g, w1, w2, gf)


def _rope_kernel(pos_ref, freq_ref, sign_ref, cos_ref, sin_ref):
    pos = pos_ref[...].astype(F32)
    freq = freq_ref[...]
    sign = sign_ref[...]
    for b in range(pos.shape[1]):
        ang = pos[:, b:b + 1] * freq
        cos_ref[b] = jnp.cos(ang)
        sin_ref[b] = jnp.sin(ang) * sign
    return


def _rope_tables(positions):
    bsz, seq = positions.shape
    half = RET_HEAD_DIM // 2
    inv_freq = ROPE_BASE ** (-jnp.arange(half, dtype=F32) / half)
    freq = jnp.concatenate([inv_freq, inv_freq])[None, :]
    sign = jnp.concatenate([-jnp.ones((half,), F32), jnp.ones((half,), F32)])[None, :]
    out = jax.ShapeDtypeStruct((bsz, seq, RET_HEAD_DIM), F32)
    return pl.pallas_call(
        _rope_kernel,
        out_shape=(out, out),
        grid=(seq // ROPE_T,),
        in_specs=[
            pl.BlockSpec((ROPE_T, bsz), lambda i: (i, 0)),
            _resident((1, RET_HEAD_DIM)),
            _resident((1, RET_HEAD_DIM)),
        ],
        out_specs=(
            pl.BlockSpec((bsz, ROPE_T, RET_HEAD_DIM), lambda i: (0, i, 0)),
            pl.BlockSpec((bsz, ROPE_T, RET_HEAD_DIM), lambda i: (0, i, 0)),
        ),
        compiler_params=pltpu.CompilerParams(
            dimension_semantics=("parallel",), vmem_limit_bytes=VMEM_LIMIT_BYTES),
        name="rope_tables",
    )(positions.T, freq, sign)


def _mixer_kernel(x_ref, cos_ref, sin_ref, gn_ref, win_ref, cw_ref, cb_ref, wg_ref, gb_ref, lam_ref,
                  dec_ref, xi_ref, zeta_ref, cd_ref, wout_ref, o_ref,
                  xb_ref, p_ref, xc_ref, a_ref, u_ref, y_ref, h_ref, kv_ref):
    nb, t = MIX_NB, MIX_T
    rows = nb * t

    @pl.when(pl.program_id(1) == 0)
    def _reset_state():
        xc_ref[:, 0:HALO, :] = jnp.zeros((nb, HALO, D_RNN), F32)
        h_ref[...] = jnp.zeros(h_ref.shape, F32)
        kv_ref[...] = jnp.zeros(kv_ref.shape, F32)

    x = x_ref[...].reshape(rows, D_MODEL)
    xb_ref[...] = _rmsnorm(x, gn_ref[...]).astype(BF16)

    xc_ref[:, HALO:HALO + t, :] = jnp.dot(
        xb_ref[...], win_ref[:, 0:D_RNN], preferred_element_type=F32).reshape(nb, t, D_RNN)
    p_ref[:, 0:D_RNN] = jnp.dot(xb_ref[...], win_ref[:, D_RNN:2 * D_RNN], preferred_element_type=F32)

    cw = cw_ref[...]
    cb = cb_ref[...]
    gb = gb_ref[...]
    lam = lam_ref[...]
    neg_c_sp = -LRU_C * (jnp.maximum(-lam, 0.0) + jnp.log1p(jnp.exp(-jnp.abs(lam))))

    def gate_body(b, carry):
        xcv = cb + cw[CONV_WIDTH - 1:CONV_WIDTH] * xc_ref[b, HALO:HALO + t, :]
        for k in range(CONV_WIDTH - 1):
            sh = CONV_WIDTH - 1 - k
            xcv = xcv + cw[k:k + 1] * xc_ref[b, HALO - sh:HALO - sh + t, :]
        gpre = jnp.dot(xcv.astype(BF16), wg_ref[...], preferred_element_type=F32) + gb
        r_gate = jax.nn.sigmoid(gpre[:, 0:D_RNN])
        i_gate = jax.nn.sigmoid(gpre[:, D_RNN:2 * D_RNN])
        log_a = r_gate * neg_c_sp
        a = jnp.exp(log_a)
        u = jnp.sqrt(jnp.tanh(-log_a) * (1.0 + a * a)) * (i_gate * xcv)
        base = pl.multiple_of(b * SCAN_PITCH, SUBLANES)
        for c in range(D_RNN // LANES):
            a_ref[c, pl.ds(base, t), :] = a[:, c * LANES:(c + 1) * LANES]
            u_ref[c, pl.ds(base, t), :] = u[:, c * LANES:(c + 1) * LANES]
        return carry

    lax.fori_loop(0, nb, gate_body, 0)
    xc_ref[:, 0:HALO, :] = xc_ref[:, t:t + HALO, :]

    n_slab = D_RNN // LANES

    def scan_body(s, hs):
        new = []
        for c in range(n_slab):
            a_t = a_ref[c, pl.ds(s, nb, stride=SCAN_PITCH), :]
            u_t = u_ref[c, pl.ds(s, nb, stride=SCAN_PITCH), :]
            h_c = a_t * hs[c] + u_t
            u_ref[c, pl.ds(s, nb, stride=SCAN_PITCH), :] = h_c
            new.append(h_c)
        return tuple(new)

    hs = lax.fori_loop(0, t, scan_body, tuple(h_ref[c] for c in range(n_slab)), unroll=8)
    for c in range(n_slab):
        h_ref[c] = hs[c]

    def rnn_out_body(b, carry):
        base = pl.multiple_of(b * SCAN_PITCH, SUBLANES)
        r0 = pl.multiple_of(b * t, t)
        gate = p_ref[pl.ds(r0, t), 0:D_RNN]
        for c in range(n_slab):
            h_c = u_ref[c, pl.ds(base, t), :]
            y_ref[pl.ds(r0, t), c * LANES:(c + 1) * LANES] = (
                h_c * jax.nn.gelu(gate[:, c * LANES:(c + 1) * LANES])).astype(BF16)
        return carry

    lax.fori_loop(0, nb, rnn_out_body, 0)

    p_ref[...] = jnp.dot(xb_ref[...], win_ref[:, 2 * D_RNN:D_IN], preferred_element_type=F32)

    def ret_body(b, carry):
        r0 = pl.multiple_of(b * t, t)
        cos = cos_ref[b]
        sin = sin_ref[b]
        for h in range(N_RET_HEADS):
            lo = h * RET_HEAD_DIM
            q = p_ref[pl.ds(r0, t), lo:lo + RET_HEAD_DIM]
            k = p_ref[pl.ds(r0, t), D_RET + lo:D_RET + lo + RET_HEAD_DIM]
            v = p_ref[pl.ds(r0, t), 2 * D_RET + lo:2 * D_RET + lo + RET_HEAD_DIM]
            g = p_ref[pl.ds(r0, t), 3 * D_RET + lo:3 * D_RET + lo + RET_HEAD_DIM]
            q = q * cos + pltpu.roll(q, RET_HEAD_DIM // 2, 1) * sin
            k = k * cos + pltpu.roll(k, RET_HEAD_DIM // 2, 1) * sin
            qb = q.astype(BF16)
            vb = v.astype(BF16)
            scores = lax.dot_general(qb, k.astype(BF16), (((1,), (1,)), ((), ())),
                                     preferred_element_type=F32) * dec_ref[h]
            state = kv_ref[b, h]
            ret = jnp.dot(scores.astype(BF16), vb, preferred_element_type=F32)
            ret = ret + jnp.dot(qb, state.astype(BF16), preferred_element_type=F32) * xi_ref[h]
            kv_ref[b, h] = cd_ref[h] * state + lax.dot_general(
                (k * zeta_ref[h]).astype(BF16), vb, (((0,), (0,)), ((), ())),
                preferred_element_type=F32)
            ret = ret * lax.rsqrt(jnp.mean(ret * ret, axis=-1, keepdims=True) + EPS)
            y_ref[pl.ds(r0, t), D_RNN + lo:D_RNN + lo + RET_HEAD_DIM] = (
                ret * (g * jax.nn.sigmoid(g))).astype(BF16)
        return carry

    lax.fori_loop(0, nb, ret_body, 0)

    out = x + jnp.dot(y_ref[...], wout_ref[...], preferred_element_type=F32)
    o_ref[...] = out.reshape(nb, t, D_MODEL)


def _retention_constants():
    n_heads, c = N_RET_HEADS, CHUNK
    log_gamma = jnp.log1p(-(2.0 ** (-5.0 - jnp.arange(n_heads, dtype=F32))))
    idx = jnp.arange(c, dtype=F32)
    rel = idx[:, None] - idx[None, :]
    decay = jnp.where(rel >= 0, jnp.exp(jnp.maximum(rel, 0.0)[None] * log_gamma[:, None, None]), 0.0)
    xi = jnp.exp((idx + 1.0)[None, :] * log_gamma[:, None])
    zeta = jnp.exp((c - 1.0 - idx)[None, :] * log_gamma[:, None])
    chunk_decay = jnp.exp(c * log_gamma)
    scale = RET_HEAD_DIM ** -0.5
    dec = decay * scale
    xi_b = jnp.broadcast_to(xi[:, :, None], (n_heads, c, RET_HEAD_DIM))
    zeta_b = jnp.broadcast_to((zeta * scale)[:, :, None], (n_heads, c, RET_HEAD_DIM))
    cd_b = jnp.broadcast_to(chunk_decay[:, None, None], (n_heads, RET_HEAD_DIM, RET_HEAD_DIM))
    return dec, xi_b, zeta_b, cd_b


def _mixer(x, cos_t, sin_t, gn, win, cw, cb, wg, gb, lam, consts, wout):
    bsz, seq, _ = x.shape
    nb, t = MIX_NB, MIX_T
    assert bsz % nb == 0 and seq % t == 0 and t == CHUNK
    dec, xi_b, zeta_b, cd_b = consts
    rows = nb * t
    n_slab = D_RNN // LANES
    tile = lambda i, j: (i, j, 0)
    hc = (N_RET_HEADS, CHUNK, RET_HEAD_DIM)
    return pl.pallas_call(
        _mixer_kernel,
        out_shape=jax.ShapeDtypeStruct(x.shape, F32),
        grid=(bsz // nb, seq // t),
        in_specs=[
            pl.BlockSpec((nb, t, D_MODEL), tile),
            pl.BlockSpec((nb, t, RET_HEAD_DIM), tile),
            pl.BlockSpec((nb, t, RET_HEAD_DIM), tile),
            _resident((1, D_MODEL)),
            _resident((D_MODEL, D_IN)),
            _resident((CONV_WIDTH, D_RNN)),
            _resident((1, D_RNN)),
            _resident((D_RNN, 2 * D_RNN)),
            _resident((1, 2 * D_RNN)),
            _resident((1, D_RNN)),
            _resident(hc),
            _resident(hc),
            _resident(hc),
            _resident(hc),
            _resident((D_MODEL, D_MODEL)),
        ],
        out_specs=pl.BlockSpec((nb, t, D_MODEL), tile),
        scratch_shapes=[
            pltpu.VMEM((rows, D_MODEL), BF16),
            pltpu.VMEM((rows, 4 * D_RET), F32),
            pltpu.VMEM((nb, HALO + t, D_RNN), F32),
            pltpu.VMEM((n_slab, nb * SCAN_PITCH, LANES), F32),
            pltpu.VMEM((n_slab, nb * SCAN_PITCH, LANES), F32),
            pltpu.VMEM((rows, D_MODEL), BF16),
            pltpu.VMEM((n_slab, nb, LANES), F32),
            pltpu.VMEM((nb, N_RET_HEADS, RET_HEAD_DIM, RET_HEAD_DIM), F32),
        ],
        compiler_params=pltpu.CompilerParams(
            dimension_semantics=("parallel", "arbitrary"), vmem_limit_bytes=VMEM_LIMIT_BYTES),
        name="mixer",
    )(x, cos_t, sin_t, gn, win, cw, cb, wg, gb, lam, dec, xi_b, zeta_b, cd_b, wout)


def _block_diag(w):
    nblk, bs, _ = w.shape
    eye = jnp.eye(nblk, dtype=w.dtype)
    return (eye[:, None, :, None] * w[:, :, None, :]).reshape(nblk * bs, nblk * bs)


def kernel(x, positions, norm_ffn1, w_ffn1_in, w_ffn1_out, norm_mix, w_in, conv_w, conv_b,
           gate_a_w, gate_a_b, gate_x_w, gate_x_b, lru_lambda, w_out, norm_ffn2,
           w_ffn2_in, w_ffn2_out, norm_final):
    bsz, seq, d = x.shape
    assert d == D_MODEL
    cos_t, sin_t = _rope_tables(positions)
    consts = _retention_constants()
    gf = norm_final[None, :]
    for l in range(DEPTH):
        x2 = _ffn(x.reshape(bsz * seq, d), norm_ffn1[l][None, :], w_ffn1_in[l].astype(BF16),
                  w_ffn1_out[l].astype(BF16), gf, final_norm=False)
        wg = jnp.concatenate([_block_diag(gate_a_w[l]), _block_diag(gate_x_w[l])], axis=1).astype(BF16)
        gb = jnp.concatenate([gate_a_b[l], gate_x_b[l]])[None, :]
        x3 = _mixer(x2.reshape(bsz, seq, d), cos_t, sin_t, norm_mix[l][None, :], w_in[l].astype(BF16),
                    conv_w[l], conv_b[l][None, :], wg, gb, lru_lambda[l][None, :], consts,
                    w_out[l].astype(BF16))
        x = _ffn(x3.reshape(bsz * seq, d), norm_ffn2[l][None, :], w_ffn2_in[l].astype(BF16),
                 w_ffn2_out[l].astype(BF16), gf, final_norm=(l == DEPTH - 1)).reshape(bsz, seq, d)
    return x
```

```python
import functools

import jax
import jax.numpy as jnp
from jax import lax
from jax.experimental import pallas as pl
from jax.experimental.pallas import tpu as pltpu

F32 = jnp.float32
BF16 = jnp.bfloat16

D_MODEL = 1024
DEPTH = 2
D_RNN = 512
D_RET = 512
N_RNN_BLOCKS = 8
RNN_BLOCK = D_RNN // N_RNN_BLOCKS
CONV_WIDTH = 4
LRU_C = 8.0
N_RET_HEADS = 4
RET_HEAD_DIM = D_RET // N_RET_HEADS
CHUNK = 128
ROPE_BASE = 10000.0
D_FF = 2816
EPS = 1e-6
D_IN = 2 * D_RNN + 4 * D_RET

LANES = 128
SUBLANES = 8
MXU_DIM = 256
VMEM_LIMIT_BYTES = 60000 * 1024

FFN_ROWS = 512
FFN_COLS = MXU_DIM
MIX_NB = 8
MIX_T = CHUNK
HALO = SUBLANES
SCAN_PITCH = MIX_T + SUBLANES
MIX_UNROLL = 4
ROPE_T = 256


def _rmsnorm(x, g):
    return x * lax.rsqrt(jnp.mean(x * x, axis=-1, keepdims=True) + EPS) * g


def _resident(shape):
    zeros = (0,) * len(shape)
    return pl.BlockSpec(shape, lambda *_: zeros, pipeline_mode=pl.Buffered(1))


def _ffn_kernel(x_ref, g_ref, w1_ref, w2_ref, gf_ref, o_ref, act_ref, *, final_norm):
    xb = _rmsnorm(x_ref[...], g_ref[...]).astype(BF16)
    for j in range(D_FF // FFN_COLS):
        lo = j * FFN_COLS
        gate = jnp.dot(xb, w1_ref[:, lo:lo + FFN_COLS], preferred_element_type=F32)
        up = jnp.dot(xb, w1_ref[:, D_FF + lo:D_FF + lo + FFN_COLS], preferred_element_type=F32)
        act_ref[:, lo:lo + FFN_COLS] = (gate * jax.nn.sigmoid(gate) * up).astype(BF16)
    y = jnp.dot(act_ref[...], w2_ref[...], preferred_element_type=F32)
    out = x_ref[...] + 0.5 * y
    if final_norm:
        out = _rmsnorm(out, gf_ref[...])
    o_ref[...] = out


def _ffn(x2d, g, w1, w2, gf, *, final_norm):
    rows = x2d.shape[0]
    assert rows % FFN_ROWS == 0 and D_FF % FFN_COLS == 0
    return pl.pallas_call(
        functools.partial(_ffn_kernel, final_norm=final_norm),
        out_shape=jax.ShapeDtypeStruct(x2d.shape, F32),
        grid=(rows // FFN_ROWS,),
        in_specs=[
            pl.BlockSpec((FFN_ROWS, D_MODEL), lambda i: (i, 0)),
            _resident((1, D_MODEL)),
            _resident((D_MODEL, 2 * D_FF)),
            _resident((D_FF, D_MODEL)),
            _resident((1, D_MODEL)),
        ],
        out_specs=pl.BlockSpec((FFN_ROWS, D_MODEL), lambda i: (i, 0)),
        scratch_shapes=[pltpu.VMEM((FFN_ROWS, D_FF), BF16)],
        compiler_params=pltpu.CompilerParams(
            dimension_semantics=("parallel",), vmem_limit_bytes=VMEM_LIMIT_BYTES),
        name="ffn_final" if final_norm else "ffn",
    )(x2d, g, w1, w2, gf)


def _rope_kernel(pos_ref, freq_ref, sign_ref, cos_ref, sin_ref):
    pos = pos_ref[...].astype(F32)
    freq = freq_ref[...]
    sign = sign_ref[...]
    for b in range(pos.shape[1]):
        ang = pos[:, b:b + 1] * freq
        cos_ref[b] = jnp.cos(ang)
        sin_ref[b] = jnp.sin(ang) * sign
    return


def _rope_tables(positions):
    bsz, seq = positions.shape
    half = RET_HEAD_DIM // 2
    inv_freq = ROPE_BASE ** (-jnp.arange(half, dtype=F32) / half)
    freq = jnp.concatenate([inv_freq, inv_freq])[None, :]
    sign = jnp.concatenate([-jnp.ones((half,), F32), jnp.ones((half,), F32)])[None, :]
    out = jax.ShapeDtypeStruct((bsz, seq, RET_HEAD_DIM), F32)
    return pl.pallas_call(
        _rope_kernel,
        out_shape=(out, out),
        grid=(seq // ROPE_T,),
        in_specs=[
            pl.BlockSpec((ROPE_T, bsz), lambda i: (i, 0)),
            _resident((1, RET_HEAD_DIM)),
            _resident((1, RET_HEAD_DIM)),
        ],
        out_specs=(
            pl.BlockSpec((bsz, ROPE_T, RET_HEAD_DIM), lambda i: (0, i, 0)),
            pl.BlockSpec((bsz, ROPE_T, RET_HEAD_DIM), lambda i: (0, i, 0)),
        ),
        compiler_params=pltpu.CompilerParams(
            dimension_semantics=("parallel",), vmem_limit_bytes=VMEM_LIMIT_BYTES),
        name="rope_tables",
    )(positions.T, freq, sign)


def _mixer_kernel(x_ref, cos_ref, sin_ref, gn_ref, win_ref, cw_ref, cb_ref, wg_ref, gb_ref, lam_ref,
                  dec_ref, xi_ref, zeta_ref, cd_ref, wout_ref, o_ref,
                  xb_ref, pg_ref, pr_ref, xc_ref, a_ref, u_ref, lhs_ref, kz_ref, y_ref, h_ref, kv_ref):
    nb, t = MIX_NB, MIX_T
    rows = nb * t
    n_slab = D_RNN // LANES

    @pl.when(pl.program_id(1) == 0)
    def _reset_state():
        xc_ref[:, 0:HALO, :] = jnp.zeros((nb, HALO, D_RNN), F32)
        h_ref[...] = jnp.zeros(h_ref.shape, F32)
        kv_ref[...] = jnp.zeros(kv_ref.shape, F32)

    xb_ref[...] = _rmsnorm(x_ref[...].reshape(rows, D_MODEL), gn_ref[...]).astype(BF16)

    xc_ref[:, HALO:HALO + t, :] = jnp.dot(
        xb_ref[...], win_ref[:, 0:D_RNN], preferred_element_type=F32).reshape(nb, t, D_RNN)
    pg_ref[...] = jnp.dot(xb_ref[...], win_ref[:, D_RNN:2 * D_RNN], preferred_element_type=F32)

    cw = cw_ref[...]
    cb = cb_ref[...]
    gb = gb_ref[...]
    lam = lam_ref[...]
    neg_c_sp = -LRU_C * (jnp.maximum(-lam, 0.0) + jnp.log1p(jnp.exp(-jnp.abs(lam))))

    def gates_and_ret_proj(b, carry):
        r0 = pl.multiple_of(b * t, t)
        pr_ref[pl.ds(r0, t), :] = jnp.dot(
            xb_ref[pl.ds(r0, t), :], win_ref[:, 2 * D_RNN:D_IN], preferred_element_type=F32)
        xcv = cb + cw[CONV_WIDTH - 1:CONV_WIDTH] * xc_ref[b, HALO:HALO + t, :]
        for k in range(CONV_WIDTH - 1):
            sh = CONV_WIDTH - 1 - k
            xcv = xcv + cw[k:k + 1] * xc_ref[b, HALO - sh:HALO - sh + t, :]
        gpre = jnp.dot(xcv.astype(BF16), wg_ref[...], preferred_element_type=F32) + gb
        r_gate = jax.nn.sigmoid(gpre[:, 0:D_RNN])
        i_gate = jax.nn.sigmoid(gpre[:, D_RNN:2 * D_RNN])
        log_a = r_gate * neg_c_sp
        a = jnp.exp(log_a)
        u = jnp.sqrt(jnp.tanh(-log_a) * (1.0 + a * a)) * (i_gate * xcv)
        base = pl.multiple_of(b * SCAN_PITCH, SUBLANES)
        for c in range(n_slab):
            a_ref[c, pl.ds(base, t), :] = a[:, c * LANES:(c + 1) * LANES]
            u_ref[c, pl.ds(base, t), :] = u[:, c * LANES:(c + 1) * LANES]
        return carry

    lax.fori_loop(0, nb, gates_and_ret_proj, 0, unroll=MIX_UNROLL)
    xc_ref[:, 0:HALO, :] = xc_ref[:, t:t + HALO, :]

    def scan_body(s, hs):
        new = []
        for c in range(n_slab):
            a_t = a_ref[c, pl.ds(s, nb, stride=SCAN_PITCH), :]
            u_t = u_ref[c, pl.ds(s, nb, stride=SCAN_PITCH), :]
            h_c = a_t * hs[c] + u_t
            u_ref[c, pl.ds(s, nb, stride=SCAN_PITCH), :] = h_c
            new.append(h_c)
        return tuple(new)

    hs = lax.fori_loop(0, t, scan_body, tuple(h_ref[c] for c in range(n_slab)), unroll=8)
    for c in range(n_slab):
        h_ref[c] = hs[c]

    def mix_out(b, carry):
        base = pl.multiple_of(b * SCAN_PITCH, SUBLANES)
        r0 = pl.multiple_of(b * t, t)
        for c in range(n_slab):
            h_c = u_ref[c, pl.ds(base, t), :]
            y_ref[pl.ds(r0, t), c * LANES:(c + 1) * LANES] = (
                h_c * jax.nn.gelu(pg_ref[pl.ds(r0, t), c * LANES:(c + 1) * LANES])).astype(BF16)
        cos = cos_ref[b]
        sin = sin_ref[b]
        for h in range(N_RET_HEADS):
            lo = h * RET_HEAD_DIM
            q = pr_ref[pl.ds(r0, t), lo:lo + RET_HEAD_DIM]
            k = pr_ref[pl.ds(r0, t), D_RET + lo:D_RET + lo + RET_HEAD_DIM]
            q = q * cos + pltpu.roll(q, RET_HEAD_DIM // 2, 1) * sin
            k = k * cos + pltpu.roll(k, RET_HEAD_DIM // 2, 1) * sin
            scores = lax.dot_general(q.astype(BF16), k.astype(BF16), (((1,), (1,)), ((), ())),
                                     preferred_element_type=F32) * dec_ref[h]
            lhs_ref[b, h, :, 0:CHUNK] = scores.astype(BF16)
            lhs_ref[b, h, :, CHUNK:CHUNK + RET_HEAD_DIM] = (q * xi_ref[h]).astype(BF16)
            kz_ref[b, h] = (k * zeta_ref[h]).astype(BF16)
        for h in range(N_RET_HEADS):
            lo = h * RET_HEAD_DIM
            vb = pr_ref[pl.ds(r0, t), 2 * D_RET + lo:2 * D_RET + lo + RET_HEAD_DIM].astype(BF16)
            g = pr_ref[pl.ds(r0, t), 3 * D_RET + lo:3 * D_RET + lo + RET_HEAD_DIM]
            state = kv_ref[b, h]
            rhs = jnp.concatenate([vb, state.astype(BF16)], axis=0)
            ret = jnp.dot(lhs_ref[b, h], rhs, preferred_element_type=F32)
            kv_ref[b, h] = cd_ref[h] * state + lax.dot_general(
                kz_ref[b, h], vb, (((0,), (0,)), ((), ())), preferred_element_type=F32)
            ret = ret * lax.rsqrt(jnp.mean(ret * ret, axis=-1, keepdims=True) + EPS)
            y_ref[pl.ds(r0, t), D_RNN + lo:D_RNN + lo + RET_HEAD_DIM] = (
                ret * (g * jax.nn.sigmoid(g))).astype(BF16)
        o_ref[b] = x_ref[b] + jnp.dot(y_ref[pl.ds(r0, t), :], wout_ref[...], preferred_element_type=F32)
        return carry

    lax.fori_loop(0, nb, mix_out, 0, unroll=MIX_UNROLL)


def _retention_constants():
    n_heads, c = N_RET_HEADS, CHUNK
    log_gamma = jnp.log1p(-(2.0 ** (-5.0 - jnp.arange(n_heads, dtype=F32))))
    idx = jnp.arange(c, dtype=F32)
    rel = idx[:, None] - idx[None, :]
    decay = jnp.where(rel >= 0, jnp.exp(jnp.maximum(rel, 0.0)[None] * log_gamma[:, None, None]), 0.0)
    xi = jnp.exp((idx + 1.0)[None, :] * log_gamma[:, None])
    zeta = jnp.exp((c - 1.0 - idx)[None, :] * log_gamma[:, None])
    chunk_decay = jnp.exp(c * log_gamma)
    scale = RET_HEAD_DIM ** -0.5
    dec = decay * scale
    xi_b = jnp.broadcast_to(xi[:, :, None], (n_heads, c, RET_HEAD_DIM))
    zeta_b = jnp.broadcast_to((zeta * scale)[:, :, None], (n_heads, c, RET_HEAD_DIM))
    cd_b = jnp.broadcast_to(chunk_decay[:, None, None], (n_heads, RET_HEAD_DIM, RET_HEAD_DIM))
    return dec, xi_b, zeta_b, cd_b


def _mixer(x, cos_t, sin_t, gn, win, cw, cb, wg, gb, lam, consts, wout):
    bsz, seq, _ = x.shape
    nb, t = MIX_NB, MIX_T
    assert bsz % nb == 0 and seq % t == 0 and t == CHUNK
    dec, xi_b, zeta_b, cd_b = consts
    rows = nb * t
    n_slab = D_RNN // LANES
    tile = lambda i, j: (i, j, 0)
    hc = (N_RET_HEADS, CHUNK, RET_HEAD_DIM)
    return pl.pallas_call(
        _mixer_kernel,
        out_shape=jax.ShapeDtypeStruct(x.shape, F32),
        grid=(bsz // nb, seq // t),
        in_specs=[
            pl.BlockSpec((nb, t, D_MODEL), tile),
            pl.BlockSpec((nb, t, RET_HEAD_DIM), tile),
            pl.BlockSpec((nb, t, RET_HEAD_DIM), tile),
            _resident((1, D_MODEL)),
            _resident((D_MODEL, D_IN)),
            _resident((CONV_WIDTH, D_RNN)),
            _resident((1, D_RNN)),
            _resident((D_RNN, 2 * D_RNN)),
            _resident((1, 2 * D_RNN)),
            _resident((1, D_RNN)),
            _resident(hc),
            _resident(hc),
            _resident(hc),
            _resident(hc),
            _resident((D_MODEL, D_MODEL)),
        ],
        out_specs=pl.BlockSpec((nb, t, D_MODEL), tile),
        scratch_shapes=[
            pltpu.VMEM((rows, D_MODEL), BF16),
            pltpu.VMEM((rows, D_RNN), F32),
            pltpu.VMEM((rows, 4 * D_RET), F32),
            pltpu.VMEM((nb, HALO + t, D_RNN), F32),
            pltpu.VMEM((n_slab, nb * SCAN_PITCH, LANES), F32),
            pltpu.VMEM((n_slab, nb * SCAN_PITCH, LANES), F32),
            pltpu.VMEM((nb, N_RET_HEADS, t, CHUNK + RET_HEAD_DIM), BF16),
            pltpu.VMEM((nb, N_RET_HEADS, t, RET_HEAD_DIM), BF16),
            pltpu.VMEM((rows, D_MODEL), BF16),
            pltpu.VMEM((n_slab, nb, LANES), F32),
            pltpu.VMEM((nb, N_RET_HEADS, RET_HEAD_DIM, RET_HEAD_DIM), F32),
        ],
        compiler_params=pltpu.CompilerParams(
            dimension_semantics=("parallel", "arbitrary"), vmem_limit_bytes=VMEM_LIMIT_BYTES),
        name="mixer",
    )(x, cos_t, sin_t, gn, win, cw, cb, wg, gb, lam, dec, xi_b, zeta_b, cd_b, wout)


def _block_diag(w):
    nblk, bs, _ = w.shape
    eye = jnp.eye(nblk, dtype=w.dtype)
    return (eye[:, None, :, None] * w[:, :, None, :]).reshape(nblk * bs, nblk * bs)


def kernel(x, positions, norm_ffn1, w_ffn1_in, w_ffn1_out, norm_mix, w_in, conv_w, conv_b,
           gate_a_w, gate_a_b, gate_x_w, gate_x_b, lru_lambda, w_out, norm_ffn2,
           w_ffn2_in, w_ffn2_out, norm_final):
    bsz, seq, d = x.shape
    assert d == D_MODEL
    cos_t, sin_t = _rope_tables(positions)
    consts = _retention_constants()
    gf = norm_final[None, :]
    for l in range(DEPTH):
        x2 = _ffn(x.reshape(bsz * seq, d), norm_ffn1[l][None, :], w_ffn1_in[l].astype(BF16),
                  w_ffn1_out[l].astype(BF16), gf, final_norm=False)
        wg = jnp.concatenate([_block_diag(gate_a_w[l]), _block_diag(gate_x_w[l])], axis=1).astype(BF16)
        gb = jnp.concatenate([gate_a_b[l], gate_x_b[l]])[None, :]
        x3 = _mixer(x2.reshape(bsz, seq, d), cos_t, sin_t, norm_mix[l][None, :], w_in[l].astype(BF16),
                    conv_w[l], conv_b[l][None, :], wg, gb, lru_lambda[l][None, :], consts,
                    w_out[l].astype(BF16))
        x = _ffn(x3.reshape(bsz * seq, d), norm_ffn2[l][None, :], w_ffn2_in[l].astype(BF16),
                 w_ffn2_out[l].astype(BF16), gf, final_norm=(l == DEPTH - 1)).reshape(bsz, seq, d)
    return x
```

```python
import functools

import jax
import jax.numpy as jnp
from jax import lax
from jax.experimental import pallas as pl
from jax.experimental.pallas import tpu as pltpu

F32 = jnp.float32
BF16 = jnp.bfloat16

D_MODEL = 1024
DEPTH = 2
D_RNN = 512
D_RET = 512
N_RNN_BLOCKS = 8
RNN_BLOCK = D_RNN // N_RNN_BLOCKS
CONV_WIDTH = 4
LRU_C = 8.0
N_RET_HEADS = 4
RET_HEAD_DIM = D_RET // N_RET_HEADS
CHUNK = 128
ROPE_BASE = 10000.0
D_FF = 2816
EPS = 1e-6
D_IN = 2 * D_RNN + 4 * D_RET

LANES = 128
SUBLANES = 8
MXU_DIM = 256
VMEM_LIMIT_BYTES = 60000 * 1024

FFN_ROWS = 1024
FFN_COLS = MXU_DIM
MIX_NB = 8
MIX_T = CHUNK
HALO = SUBLANES
SCAN_PITCH = MIX_T + SUBLANES
MIX_PAIR = 2
MIX_UNROLL = 4
ROPE_T = 256


def _rmsnorm(x, g):
    return x * lax.rsqrt(jnp.mean(x * x, axis=-1, keepdims=True) + EPS) * g


def _resident(shape):
    zeros = (0,) * len(shape)
    return pl.BlockSpec(shape, lambda *_: zeros, pipeline_mode=pl.Buffered(1))


def _ffn_kernel(x_ref, g_ref, w1_ref, w2_ref, gf_ref, o_ref, act_ref, *, final_norm):
    xb = _rmsnorm(x_ref[...], g_ref[...]).astype(BF16)
    for j in range(D_FF // FFN_COLS):
        lo = j * FFN_COLS
        gate = jnp.dot(xb, w1_ref[:, lo:lo + FFN_COLS], preferred_element_type=F32)
        up = jnp.dot(xb, w1_ref[:, D_FF + lo:D_FF + lo + FFN_COLS], preferred_element_type=F32)
        act_ref[:, lo:lo + FFN_COLS] = (gate * jax.nn.sigmoid(gate) * up).astype(BF16)
    y = jnp.dot(act_ref[...], w2_ref[...], preferred_element_type=F32)
    out = x_ref[...] + 0.5 * y
    if final_norm:
        out = _rmsnorm(out, gf_ref[...])
    o_ref[...] = out


def _ffn(x2d, g, w1, w2, gf, *, final_norm):
    rows = x2d.shape[0]
    assert rows % FFN_ROWS == 0 and D_FF % FFN_COLS == 0
    return pl.pallas_call(
        functools.partial(_ffn_kernel, final_norm=final_norm),
        out_shape=jax.ShapeDtypeStruct(x2d.shape, F32),
        grid=(rows // FFN_ROWS,),
        in_specs=[
            pl.BlockSpec((FFN_ROWS, D_MODEL), lambda i: (i, 0)),
            _resident((1, D_MODEL)),
            _resident((D_MODEL, 2 * D_FF)),
            _resident((D_FF, D_MODEL)),
            _resident((1, D_MODEL)),
        ],
        out_specs=pl.BlockSpec((FFN_ROWS, D_MODEL), lambda i: (i, 0)),
        scratch_shapes=[pltpu.VMEM((FFN_ROWS, D_FF), BF16)],
        compiler_params=pltpu.CompilerParams(
            dimension_semantics=("parallel",), vmem_limit_bytes=VMEM_LIMIT_BYTES),
        name="ffn_final" if final_norm else "ffn",
    )(x2d, g, w1, w2, gf)


def _rope_kernel(pos_ref, freq_ref, sign_ref, cos_ref, sin_ref):
    pos = pos_ref[...].astype(F32)
    freq = freq_ref[...]
    sign = sign_ref[...]
    nseq = pos.shape[1]
    half = RET_HEAD_DIM // 2
    low = lax.broadcasted_iota(jnp.int32, (pos.shape[0], RET_HEAD_DIM), 1) < half
    for b in range(nseq // 2):
        ang = jnp.where(low, pos[:, b:b + 1], pos[:, b + nseq // 2:b + nseq // 2 + 1]) * freq
        c = jnp.cos(ang)
        s = jnp.sin(ang)
        c_sw = pltpu.roll(c, half, 1)
        s_sw = pltpu.roll(s, half, 1)
        cos_ref[b] = jnp.where(low, c, c_sw)
        cos_ref[b + nseq // 2] = jnp.where(low, c_sw, c)
        sin_ref[b] = jnp.where(low, s, s_sw) * sign
        sin_ref[b + nseq // 2] = jnp.where(low, s_sw, s) * sign


def _rope_tables(positions):
    bsz, seq = positions.shape
    half = RET_HEAD_DIM // 2
    inv_freq = ROPE_BASE ** (-jnp.arange(half, dtype=F32) / half)
    freq = jnp.concatenate([inv_freq, inv_freq])[None, :]
    sign = jnp.concatenate([-jnp.ones((half,), F32), jnp.ones((half,), F32)])[None, :]
    out = jax.ShapeDtypeStruct((bsz, seq, RET_HEAD_DIM), F32)
    return pl.pallas_call(
        _rope_kernel,
        out_shape=(out, out),
        grid=(seq // ROPE_T,),
        in_specs=[
            pl.BlockSpec((ROPE_T, bsz), lambda i: (i, 0)),
            _resident((1, RET_HEAD_DIM)),
            _resident((1, RET_HEAD_DIM)),
        ],
        out_specs=(
            pl.BlockSpec((bsz, ROPE_T, RET_HEAD_DIM), lambda i: (0, i, 0)),
            pl.BlockSpec((bsz, ROPE_T, RET_HEAD_DIM), lambda i: (0, i, 0)),
        ),
        compiler_params=pltpu.CompilerParams(
            dimension_semantics=("parallel",), vmem_limit_bytes=VMEM_LIMIT_BYTES),
        name="rope_tables",
    )(positions.T, freq, sign)


def _mixer_kernel(x_ref, cos_ref, sin_ref, gn_ref, win_ref, cw_ref, cb_ref, wg_ref, gb_ref, lam_ref,
                  dec_ref, xi_ref, zeta_ref, cd_ref, wout_ref, o_ref,
                  xb_ref, pg_ref, pr_ref, xc_ref, a_ref, u_ref, lhs_ref, kz_ref, y_ref, h_ref, kv_ref):
    nb, t = MIX_NB, MIX_T
    rows = nb * t
    n_slab = D_RNN // LANES

    @pl.when(pl.program_id(1) == 0)
    def _reset_state():
        xc_ref[:, 0:HALO, :] = jnp.zeros((nb, HALO, D_RNN), F32)
        h_ref[...] = jnp.zeros(h_ref.shape, F32)
        kv_ref[...] = jnp.zeros(kv_ref.shape, F32)

    xb_ref[...] = _rmsnorm(x_ref[...].reshape(rows, D_MODEL), gn_ref[...]).astype(BF16)

    xc_ref[:, HALO:HALO + t, :] = jnp.dot(
        xb_ref[...], win_ref[:, 0:D_RNN], preferred_element_type=F32).reshape(nb, t, D_RNN)
    pg_ref[...] = jnp.dot(xb_ref[...], win_ref[:, D_RNN:2 * D_RNN], preferred_element_type=F32)

    cw = cw_ref[...]
    cb = cb_ref[...]
    gb = gb_ref[...]
    lam = lam_ref[...]
    neg_c_sp = -LRU_C * (jnp.maximum(-lam, 0.0) + jnp.log1p(jnp.exp(-jnp.abs(lam))))

    def gates_and_ret_proj(j, carry):
        r0 = pl.multiple_of(j * (MIX_PAIR * t), MIX_PAIR * t)
        pr_ref[pl.ds(r0, MIX_PAIR * t), :] = jnp.dot(
            xb_ref[pl.ds(r0, MIX_PAIR * t), :], win_ref[:, 2 * D_RNN:D_IN], preferred_element_type=F32)
        parts = []
        for s in range(MIX_PAIR):
            b = j * MIX_PAIR + s
            xcv_s = cb + cw[CONV_WIDTH - 1:CONV_WIDTH] * xc_ref[b, HALO:HALO + t, :]
            for k in range(CONV_WIDTH - 1):
                sh = CONV_WIDTH - 1 - k
                xcv_s = xcv_s + cw[k:k + 1] * xc_ref[b, HALO - sh:HALO - sh + t, :]
            parts.append(xcv_s)
        xcv = jnp.concatenate(parts, axis=0)
        gpre = jnp.dot(xcv.astype(BF16), wg_ref[:, 0:2 * D_RNN], preferred_element_type=F32) + gb
        r_gate = jax.nn.sigmoid(gpre[:, 0:D_RNN])
        i_gate = jax.nn.sigmoid(gpre[:, D_RNN:2 * D_RNN])
        log_a = r_gate * neg_c_sp
        a = jnp.exp(log_a)
        u = jnp.sqrt(jnp.tanh(-log_a) * (1.0 + a * a)) * (i_gate * xcv)
        for s in range(MIX_PAIR):
            base = pl.multiple_of((j * MIX_PAIR + s) * SCAN_PITCH, SUBLANES)
            for c in range(n_slab):
                a_ref[c, pl.ds(base, t), :] = a[s * t:(s + 1) * t, c * LANES:(c + 1) * LANES]
                u_ref[c, pl.ds(base, t), :] = u[s * t:(s + 1) * t, c * LANES:(c + 1) * LANES]
        return carry

    lax.fori_loop(0, nb // MIX_PAIR, gates_and_ret_proj, 0, unroll=nb // MIX_PAIR)
    xc_ref[:, 0:HALO, :] = xc_ref[:, t:t + HALO, :]

    def scan_body(s, hs):
        new = []
        for c in range(n_slab):
            a_t = a_ref[c, pl.ds(s, nb, stride=SCAN_PITCH), :]
            u_t = u_ref[c, pl.ds(s, nb, stride=SCAN_PITCH), :]
            h_c = a_t * hs[c] + u_t
            u_ref[c, pl.ds(s, nb, stride=SCAN_PITCH), :] = h_c
            new.append(h_c)
        return tuple(new)

    hs = lax.fori_loop(0, t, scan_body, tuple(h_ref[c] for c in range(n_slab)), unroll=8)
    for c in range(n_slab):
        h_ref[c] = hs[c]

    def mix_out(b, carry):
        base = pl.multiple_of(b * SCAN_PITCH, SUBLANES)
        r0 = pl.multiple_of(b * t, t)
        for c in range(n_slab):
            h_c = u_ref[c, pl.ds(base, t), :]
            y_ref[pl.ds(r0, t), c * LANES:(c + 1) * LANES] = (
                h_c * jax.nn.gelu(pg_ref[pl.ds(r0, t), c * LANES:(c + 1) * LANES])).astype(BF16)
        cos = cos_ref[b]
        sin = sin_ref[b]
        for h in range(N_RET_HEADS):
            lo = h * RET_HEAD_DIM
            q = pr_ref[pl.ds(r0, t), lo:lo + RET_HEAD_DIM]
            k = pr_ref[pl.ds(r0, t), D_RET + lo:D_RET + lo + RET_HEAD_DIM]
            q = q * cos + pltpu.roll(q, RET_HEAD_DIM // 2, 1) * sin
            k = k * cos + pltpu.roll(k, RET_HEAD_DIM // 2, 1) * sin
            scores = lax.dot_general(q.astype(BF16), k.astype(BF16), (((1,), (1,)), ((), ())),
                                     preferred_element_type=F32) * dec_ref[h]
            lhs_ref[b, h, :, 0:CHUNK] = scores.astype(BF16)
            lhs_ref[b, h, :, CHUNK:CHUNK + RET_HEAD_DIM] = (q * xi_ref[h]).astype(BF16)
            kz_ref[b, h] = (k * zeta_ref[h]).astype(BF16)
        for h in range(N_RET_HEADS):
            lo = h * RET_HEAD_DIM
            vb = pr_ref[pl.ds(r0, t), 2 * D_RET + lo:2 * D_RET + lo + RET_HEAD_DIM].astype(BF16)
            g = pr_ref[pl.ds(r0, t), 3 * D_RET + lo:3 * D_RET + lo + RET_HEAD_DIM]
            state = kv_ref[b, h]
            rhs = jnp.concatenate([vb, state.astype(BF16)], axis=0)
            ret = jnp.dot(lhs_ref[b, h], rhs, preferred_element_type=F32)
            kv_ref[b, h] = cd_ref[h] * state + lax.dot_general(
                kz_ref[b, h], vb, (((0,), (0,)), ((), ())), preferred_element_type=F32)
            ret = ret * lax.rsqrt(jnp.mean(ret * ret, axis=-1, keepdims=True) + EPS)
            y_ref[pl.ds(r0, t), D_RNN + lo:D_RNN + lo + RET_HEAD_DIM] = (
                ret * (g * jax.nn.sigmoid(g))).astype(BF16)
        o_ref[b] = x_ref[b] + jnp.dot(
            y_ref[pl.ds(r0, t), :], wout_ref[:, 0:D_MODEL], preferred_element_type=F32)
        return carry

    lax.fori_loop(0, nb, mix_out, 0, unroll=MIX_UNROLL)


def _retention_constants():
    n_heads, c = N_RET_HEADS, CHUNK
    log_gamma = jnp.log1p(-(2.0 ** (-5.0 - jnp.arange(n_heads, dtype=F32))))
    idx = jnp.arange(c, dtype=F32)
    rel = idx[:, None] - idx[None, :]
    decay = jnp.where(rel >= 0, jnp.exp(jnp.maximum(rel, 0.0)[None] * log_gamma[:, None, None]), 0.0)
    xi = jnp.exp((idx + 1.0)[None, :] * log_gamma[:, None])
    zeta = jnp.exp((c - 1.0 - idx)[None, :] * log_gamma[:, None])
    chunk_decay = jnp.exp(c * log_gamma)
    scale = RET_HEAD_DIM ** -0.5
    dec = decay * scale
    xi_b = jnp.broadcast_to(xi[:, :, None], (n_heads, c, RET_HEAD_DIM))
    zeta_b = jnp.broadcast_to((zeta * scale)[:, :, None], (n_heads, c, RET_HEAD_DIM))
    cd_b = jnp.broadcast_to(chunk_decay[:, None, None], (n_heads, RET_HEAD_DIM, RET_HEAD_DIM))
    return dec, xi_b, zeta_b, cd_b


def _mixer(x, cos_t, sin_t, gn, win, cw, cb, wg, gb, lam, consts, wout):
    bsz, seq, _ = x.shape
    nb, t = MIX_NB, MIX_T
    assert bsz % nb == 0 and seq % t == 0 and t == CHUNK
    dec, xi_b, zeta_b, cd_b = consts
    rows = nb * t
    n_slab = D_RNN // LANES
    tile = lambda i, j: (i, j, 0)
    hc = (N_RET_HEADS, CHUNK, RET_HEAD_DIM)
    return pl.pallas_call(
        _mixer_kernel,
        out_shape=jax.ShapeDtypeStruct(x.shape, F32),
        grid=(bsz // nb, seq // t),
        in_specs=[
            pl.BlockSpec((nb, t, D_MODEL), tile),
            pl.BlockSpec((nb, t, RET_HEAD_DIM), tile),
            pl.BlockSpec((nb, t, RET_HEAD_DIM), tile),
            _resident((1, D_MODEL)),
            _resident((D_MODEL, D_IN + LANES)),
            _resident((CONV_WIDTH, D_RNN)),
            _resident((1, D_RNN)),
            _resident((D_RNN, 2 * D_RNN + LANES)),
            _resident((1, 2 * D_RNN)),
            _resident((1, D_RNN)),
            _resident(hc),
            _resident(hc),
            _resident(hc),
            _resident(hc),
            _resident((D_MODEL, D_MODEL + LANES)),
        ],
        out_specs=pl.BlockSpec((nb, t, D_MODEL), tile),
        scratch_shapes=[
            pltpu.VMEM((rows, D_MODEL), BF16),
            pltpu.VMEM((rows, D_RNN), F32),
            pltpu.VMEM((rows, 4 * D_RET), F32),
            pltpu.VMEM((nb, HALO + t, D_RNN), F32),
            pltpu.VMEM((n_slab, nb * SCAN_PITCH, LANES), F32),
            pltpu.VMEM((n_slab, nb * SCAN_PITCH, LANES), F32),
            pltpu.VMEM((nb, N_RET_HEADS, t, CHUNK + RET_HEAD_DIM), BF16),
            pltpu.VMEM((nb, N_RET_HEADS, t, RET_HEAD_DIM), BF16),
            pltpu.VMEM((rows, D_MODEL), BF16),
            pltpu.VMEM((n_slab, nb, LANES), F32),
            pltpu.VMEM((nb, N_RET_HEADS, RET_HEAD_DIM, RET_HEAD_DIM), F32),
        ],
        compiler_params=pltpu.CompilerParams(
            dimension_semantics=("parallel", "arbitrary"), vmem_limit_bytes=VMEM_LIMIT_BYTES),
        name="mixer",
    )(x, cos_t, sin_t, gn, _pad_lanes(win), cw, cb, _pad_lanes(wg), gb, lam, dec, xi_b, zeta_b, cd_b,
      _pad_lanes(wout))


def _pad_lanes(w):
    return jnp.pad(w, ((0, 0), (0, LANES)))


def _block_diag(w):
    nblk, bs, _ = w.shape
    eye = jnp.eye(nblk, dtype=w.dtype)
    return (eye[:, None, :, None] * w[:, :, None, :]).reshape(nblk * bs, nblk * bs)


def kernel(x, positions, norm_ffn1, w_ffn1_in, w_ffn1_out, norm_mix, w_in, conv_w, conv_b,
           gate_a_w, gate_a_b, gate_x_w, gate_x_b, lru_lambda, w_out, norm_ffn2,
           w_ffn2_in, w_ffn2_out, norm_final):
    bsz, seq, d = x.shape
    assert d == D_MODEL
    cos_t, sin_t = _rope_tables(positions)
    consts = _retention_constants()
    gf = norm_final[None, :]
    for l in range(DEPTH):
        x2 = _ffn(x.reshape(bsz * seq, d), norm_ffn1[l][None, :], w_ffn1_in[l].astype(BF16),
                  w_ffn1_out[l].astype(BF16), gf, final_norm=False)
        wg = jnp.concatenate([_block_diag(gate_a_w[l]), _block_diag(gate_x_w[l])], axis=1).astype(BF16)
        gb = jnp.concatenate([gate_a_b[l], gate_x_b[l]])[None, :]
        x3 = _mixer(x2.reshape(bsz, seq, d), cos_t, sin_t, norm_mix[l][None, :], w_in[l].astype(BF16),
                    conv_w[l], conv_b[l][None, :], wg, gb, lru_lambda[l][None, :], consts,
                    w_out[l].astype(BF16))
        x = _ffn(x3.reshape(bsz * seq, d), norm_ffn2[l][None, :], w_ffn2_in[l].astype(BF16),
                 w_ffn2_out[l].astype(BF16), gf, final_norm=(l == DEPTH - 1)).reshape(bsz, seq, d)
    return x
```

```python
import functools

import jax
import jax.numpy as jnp
from jax import lax
from jax.experimental import pallas as pl
from jax.experimental.pallas import tpu as pltpu

F32 = jnp.float32
BF16 = jnp.bfloat16

D_MODEL = 1024
DEPTH = 2
D_RNN = 512
D_RET = 512
N_RNN_BLOCKS = 8
RNN_BLOCK = D_RNN // N_RNN_BLOCKS
CONV_WIDTH = 4
LRU_C = 8.0
N_RET_HEADS = 4
RET_HEAD_DIM = D_RET // N_RET_HEADS
CHUNK = 128
ROPE_BASE = 10000.0
D_FF = 2816
EPS = 1e-6
D_IN = 2 * D_RNN + 4 * D_RET

LANES = 128
SUBLANES = 8
MXU_DIM = 256
VMEM_LIMIT_BYTES = 60000 * 1024

FFN_ROWS = 1024
FFN_COLS = MXU_DIM
MIX_NB = 8
MIX_T = CHUNK
HALO = SUBLANES
MIX_PAIR = 2
MIX_UNROLL = 4
ROPE_T = 256


def _rmsnorm(x, g):
    return x * lax.rsqrt(jnp.mean(x * x, axis=-1, keepdims=True) + EPS) * g


def _resident(shape, layer=None):
    zeros = (0,) * len(shape)
    if layer is None:
        return pl.BlockSpec(shape, lambda *_: zeros, pipeline_mode=pl.Buffered(1))
    return pl.BlockSpec((None,) + tuple(shape), lambda *_: (layer,) + zeros, pipeline_mode=pl.Buffered(1))


def _ffn_kernel(x_ref, g_ref, w1_ref, w2_ref, gf_ref, o_ref, act_ref, *, final_norm):
    xb = _rmsnorm(x_ref[...], g_ref[...]).astype(BF16)
    for j in range(D_FF // FFN_COLS):
        lo = j * FFN_COLS
        gate = jnp.dot(xb, w1_ref[:, lo:lo + FFN_COLS], preferred_element_type=F32)
        up = jnp.dot(xb, w1_ref[:, D_FF + lo:D_FF + lo + FFN_COLS], preferred_element_type=F32)
        act_ref[:, lo:lo + FFN_COLS] = (gate * jax.nn.sigmoid(gate) * up).astype(BF16)
    y = jnp.dot(act_ref[...], w2_ref[...], preferred_element_type=F32)
    out = x_ref[...] + 0.5 * y
    if final_norm:
        out = _rmsnorm(out, gf_ref[...])
    o_ref[...] = out


def _ffn(x2d, g, w1, w2, gf, *, layer, final_norm):
    rows = x2d.shape[0]
    assert rows % FFN_ROWS == 0 and D_FF % FFN_COLS == 0
    return pl.pallas_call(
        functools.partial(_ffn_kernel, final_norm=final_norm),
        out_shape=jax.ShapeDtypeStruct(x2d.shape, F32),
        grid=(rows // FFN_ROWS,),
        in_specs=[
            pl.BlockSpec((FFN_ROWS, D_MODEL), lambda i: (i, 0)),
            _resident((1, D_MODEL)),
            _resident((D_MODEL, 2 * D_FF), layer),
            _resident((D_FF, D_MODEL), layer),
            _resident((1, D_MODEL)),
        ],
        out_specs=pl.BlockSpec((FFN_ROWS, D_MODEL), lambda i: (i, 0)),
        scratch_shapes=[pltpu.VMEM((FFN_ROWS, D_FF), BF16)],
        compiler_params=pltpu.CompilerParams(
            dimension_semantics=("parallel",), vmem_limit_bytes=VMEM_LIMIT_BYTES),
        name="ffn_final" if final_norm else "ffn",
    )(x2d, g, w1, w2, gf)


def _rope_kernel(pos_ref, freq_ref, sign_ref, cos_ref, sin_ref):
    pos = pos_ref[...].astype(F32)
    freq = freq_ref[...]
    sign = sign_ref[...]
    nseq = pos.shape[1]
    half = RET_HEAD_DIM // 2
    low = lax.broadcasted_iota(jnp.int32, (pos.shape[0], RET_HEAD_DIM), 1) < half
    for b in range(nseq // 2):
        ang = jnp.where(low, pos[:, b:b + 1], pos[:, b + nseq // 2:b + nseq // 2 + 1]) * freq
        c = jnp.cos(ang)
        s = jnp.sin(ang)
        c_sw = pltpu.roll(c, half, 1)
        s_sw = pltpu.roll(s, half, 1)
        cos_ref[b] = jnp.where(low, c, c_sw)
        cos_ref[b + nseq // 2] = jnp.where(low, c_sw, c)
        sin_ref[b] = jnp.where(low, s, s_sw) * sign
        sin_ref[b + nseq // 2] = jnp.where(low, s_sw, s) * sign


def _rope_tables(positions):
    bsz, seq = positions.shape
    half = RET_HEAD_DIM // 2
    inv_freq = ROPE_BASE ** (-jnp.arange(half, dtype=F32) / half)
    freq = jnp.concatenate([inv_freq, inv_freq])[None, :]
    sign = jnp.concatenate([-jnp.ones((half,), F32), jnp.ones((half,), F32)])[None, :]
    out = jax.ShapeDtypeStruct((bsz, seq, RET_HEAD_DIM), F32)
    return pl.pallas_call(
        _rope_kernel,
        out_shape=(out, out),
        grid=(seq // ROPE_T,),
        in_specs=[
            pl.BlockSpec((ROPE_T, bsz), lambda i: (i, 0)),
            _resident((1, RET_HEAD_DIM)),
            _resident((1, RET_HEAD_DIM)),
        ],
        out_specs=(
            pl.BlockSpec((bsz, ROPE_T, RET_HEAD_DIM), lambda i: (0, i, 0)),
            pl.BlockSpec((bsz, ROPE_T, RET_HEAD_DIM), lambda i: (0, i, 0)),
        ),
        compiler_params=pltpu.CompilerParams(
            dimension_semantics=("parallel",), vmem_limit_bytes=VMEM_LIMIT_BYTES),
        name="rope_tables",
    )(positions.T, freq, sign)


def _mixer_kernel(x_ref, cos_ref, sin_ref, gn_ref, win_ref, cw_ref, cb_ref, wg_ref, gb_ref, lam_ref,
                  dec_ref, xi_ref, zeta_ref, cd_ref, wout_ref, o_ref,
                  xb_ref, pg_ref, pr_ref, xc_ref, a_ref, u_ref, lhs_ref, kz_ref, y_ref, h_ref, kv_ref):
    nb, t = MIX_NB, MIX_T
    rows = nb * t
    n_slab = D_RNN // LANES

    @pl.when(pl.program_id(1) == 0)
    def _reset_state():
        xc_ref[:, 0:HALO, :] = jnp.zeros((nb, HALO, D_RNN), F32)
        h_ref[...] = jnp.zeros(h_ref.shape, F32)
        kv_ref[...] = jnp.zeros(kv_ref.shape, F32)

    xb_ref[...] = _rmsnorm(x_ref[...].reshape(rows, D_MODEL), gn_ref[...]).astype(BF16)

    xc_ref[:, HALO:HALO + t, :] = jnp.dot(
        xb_ref[...], win_ref[:, 0:D_RNN], preferred_element_type=F32).reshape(nb, t, D_RNN)
    pg_ref[...] = jnp.dot(xb_ref[...], win_ref[:, D_RNN:2 * D_RNN], preferred_element_type=F32)

    cw = cw_ref[...]
    cb = cb_ref[...]
    gb = gb_ref[...]
    lam = lam_ref[...]
    neg_c_sp = -LRU_C * (jnp.maximum(-lam, 0.0) + jnp.log1p(jnp.exp(-jnp.abs(lam))))

    def gates_and_ret_proj(j, carry):
        r0 = pl.multiple_of(j * (MIX_PAIR * t), MIX_PAIR * t)
        parts = []
        for s in range(MIX_PAIR):
            b = j * MIX_PAIR + s
            xcv_s = cb + cw[CONV_WIDTH - 1:CONV_WIDTH] * xc_ref[b, HALO:HALO + t, :]
            for k in range(CONV_WIDTH - 1):
                sh = CONV_WIDTH - 1 - k
                xcv_s = xcv_s + cw[k:k + 1] * xc_ref[b, HALO - sh:HALO - sh + t, :]
            parts.append(xcv_s)
        xcv = jnp.concatenate(parts, axis=0)
        gpre = jnp.dot(xcv.astype(BF16), wg_ref[:, 0:2 * D_RNN], preferred_element_type=F32) + gb
        r_gate = jax.nn.sigmoid(gpre[:, 0:D_RNN])
        i_gate = jax.nn.sigmoid(gpre[:, D_RNN:2 * D_RNN])
        log_a = r_gate * neg_c_sp
        a = jnp.exp(log_a)
        u = jnp.sqrt(jnp.tanh(-log_a) * (1.0 + a * a)) * (i_gate * xcv)
        pr_ref[pl.ds(r0, MIX_PAIR * t), :] = jnp.dot(
            xb_ref[pl.ds(r0, MIX_PAIR * t), :], win_ref[:, 2 * D_RNN:D_IN], preferred_element_type=F32)
        for s in range(MIX_PAIR):
            b = j * MIX_PAIR + s
            for c in range(n_slab):
                a_ref[c, pl.ds(b, t, stride=nb), :] = a[s * t:(s + 1) * t, c * LANES:(c + 1) * LANES]
                u_ref[c, pl.ds(b, t, stride=nb), :] = u[s * t:(s + 1) * t, c * LANES:(c + 1) * LANES]
        return carry

    lax.fori_loop(0, nb // MIX_PAIR, gates_and_ret_proj, 0, unroll=nb // MIX_PAIR)
    xc_ref[:, 0:HALO, :] = xc_ref[:, t:t + HALO, :]

    def scan_body(s, hs):
        new = []
        for c in range(n_slab):
            row = pl.multiple_of(s * nb, nb)
            h_c = a_ref[c, pl.ds(row, nb), :] * hs[c] + u_ref[c, pl.ds(row, nb), :]
            u_ref[c, pl.ds(row, nb), :] = h_c
            new.append(h_c)
        return tuple(new)

    hs = lax.fori_loop(0, t, scan_body, tuple(h_ref[c] for c in range(n_slab)), unroll=8)
    for c in range(n_slab):
        h_ref[c] = hs[c]

    def mix_out(b, carry):
        r0 = pl.multiple_of(b * t, t)
        for c in range(n_slab):
            h_c = u_ref[c, pl.ds(b, t, stride=nb), :]
            y_ref[pl.ds(r0, t), c * LANES:(c + 1) * LANES] = (
                h_c * jax.nn.gelu(pg_ref[pl.ds(r0, t), c * LANES:(c + 1) * LANES])).astype(BF16)
        cos = cos_ref[b]
        sin = sin_ref[b]
        for h in range(N_RET_HEADS):
            lo = h * RET_HEAD_DIM
            q = pr_ref[pl.ds(r0, t), lo:lo + RET_HEAD_DIM]
            k = pr_ref[pl.ds(r0, t), D_RET + lo:D_RET + lo + RET_HEAD_DIM]
            q = q * cos + pltpu.roll(q, RET_HEAD_DIM // 2, 1) * sin
            k = k * cos + pltpu.roll(k, RET_HEAD_DIM // 2, 1) * sin
            scores = lax.dot_general(q.astype(BF16), k.astype(BF16), (((1,), (1,)), ((), ())),
                                     preferred_element_type=F32) * dec_ref[h]
            lhs_ref[b, h, :, 0:CHUNK] = scores.astype(BF16)
            lhs_ref[b, h, :, CHUNK:CHUNK + RET_HEAD_DIM] = (q * xi_ref[h]).astype(BF16)
            kz_ref[b, h] = (k * zeta_ref[h]).astype(BF16)
        for h in range(N_RET_HEADS):
            lo = h * RET_HEAD_DIM
            vb = pr_ref[pl.ds(r0, t), 2 * D_RET + lo:2 * D_RET + lo + RET_HEAD_DIM].astype(BF16)
            g = pr_ref[pl.ds(r0, t), 3 * D_RET + lo:3 * D_RET + lo + RET_HEAD_DIM]
            state = kv_ref[b, h]
            rhs = jnp.concatenate([vb, state.astype(BF16)], axis=0)
            ret = jnp.dot(lhs_ref[b, h], rhs, preferred_element_type=F32)
            kv_ref[b, h] = cd_ref[h] * state + lax.dot_general(
                kz_ref[b, h], vb, (((0,), (0,)), ((), ())), preferred_element_type=F32)
            ret = ret * lax.rsqrt(jnp.mean(ret * ret, axis=-1, keepdims=True) + EPS)
            y_ref[pl.ds(r0, t), D_RNN + lo:D_RNN + lo + RET_HEAD_DIM] = (
                ret * (g * jax.nn.sigmoid(g))).astype(BF16)
        acc = x_ref[b]
        for kt in range(D_MODEL // MXU_DIM):
            acc = acc + jnp.dot(y_ref[pl.ds(r0, t), kt * MXU_DIM:(kt + 1) * MXU_DIM],
                                wout_ref[kt * MXU_DIM:(kt + 1) * MXU_DIM, 0:D_MODEL],
                                preferred_element_type=F32)
        o_ref[b] = acc
        return carry

    lax.fori_loop(0, nb, mix_out, 0, unroll=MIX_UNROLL)


def _retention_constants():
    n_heads, c = N_RET_HEADS, CHUNK
    log_gamma = jnp.log1p(-(2.0 ** (-5.0 - jnp.arange(n_heads, dtype=F32))))
    idx = jnp.arange(c, dtype=F32)
    rel = idx[:, None] - idx[None, :]
    decay = jnp.where(rel >= 0, jnp.exp(jnp.maximum(rel, 0.0)[None] * log_gamma[:, None, None]), 0.0)
    xi = jnp.exp((idx + 1.0)[None, :] * log_gamma[:, None])
    zeta = jnp.exp((c - 1.0 - idx)[None, :] * log_gamma[:, None])
    chunk_decay = jnp.exp(c * log_gamma)
    scale = RET_HEAD_DIM ** -0.5
    dec = decay * scale
    xi_b = jnp.broadcast_to(xi[:, :, None], (n_heads, c, RET_HEAD_DIM))
    zeta_b = jnp.broadcast_to((zeta * scale)[:, :, None], (n_heads, c, RET_HEAD_DIM))
    cd_b = jnp.broadcast_to(chunk_decay[:, None, None], (n_heads, RET_HEAD_DIM, RET_HEAD_DIM))
    return dec, xi_b, zeta_b, cd_b


def _mixer(x, cos_t, sin_t, gn, win, cw, cb, wg, gb, lam, consts, wout, *, layer):
    bsz, seq, _ = x.shape
    nb, t = MIX_NB, MIX_T
    assert bsz % nb == 0 and seq % t == 0 and t == CHUNK
    dec, xi_b, zeta_b, cd_b = consts
    rows = nb * t
    n_slab = D_RNN // LANES
    tile = lambda i, j: (i, j, 0)
    hc = (N_RET_HEADS, CHUNK, RET_HEAD_DIM)
    return pl.pallas_call(
        _mixer_kernel,
        out_shape=jax.ShapeDtypeStruct(x.shape, F32),
        grid=(bsz // nb, seq // t),
        in_specs=[
            pl.BlockSpec((nb, t, D_MODEL), tile),
            pl.BlockSpec((nb, t, RET_HEAD_DIM), tile),
            pl.BlockSpec((nb, t, RET_HEAD_DIM), tile),
            _resident((1, D_MODEL)),
            _resident((D_MODEL, D_IN + LANES), layer),
            _resident((CONV_WIDTH, D_RNN)),
            _resident((1, D_RNN)),
            _resident((D_RNN, 2 * D_RNN + LANES), layer),
            _resident((1, 2 * D_RNN)),
            _resident((1, D_RNN)),
            _resident(hc),
            _resident(hc),
            _resident(hc),
            _resident(hc),
            _resident((D_MODEL, D_MODEL + LANES), layer),
        ],
        out_specs=pl.BlockSpec((nb, t, D_MODEL), tile),
        scratch_shapes=[
            pltpu.VMEM((rows, D_MODEL), BF16),
            pltpu.VMEM((rows, D_RNN), F32),
            pltpu.VMEM((rows, 4 * D_RET), F32),
            pltpu.VMEM((nb, HALO + t, D_RNN), F32),
            pltpu.VMEM((n_slab, t * nb, LANES), F32),
            pltpu.VMEM((n_slab, t * nb, LANES), F32),
            pltpu.VMEM((nb, N_RET_HEADS, t, CHUNK + RET_HEAD_DIM), BF16),
            pltpu.VMEM((nb, N_RET_HEADS, t, RET_HEAD_DIM), BF16),
            pltpu.VMEM((rows, D_MODEL), BF16),
            pltpu.VMEM((n_slab, nb, LANES), F32),
            pltpu.VMEM((nb, N_RET_HEADS, RET_HEAD_DIM, RET_HEAD_DIM), F32),
        ],
        compiler_params=pltpu.CompilerParams(
            dimension_semantics=("parallel", "arbitrary"), vmem_limit_bytes=VMEM_LIMIT_BYTES),
        name="mixer",
    )(x, cos_t, sin_t, gn, win, cw, cb, wg, gb, lam, dec, xi_b, zeta_b, cd_b, wout)


def _pad_lanes(w):
    return jnp.pad(w, [(0, 0)] * (w.ndim - 1) + [(0, LANES)])


def _block_diag(w):
    nl, nblk, bs, _ = w.shape
    eye = jnp.eye(nblk, dtype=w.dtype)
    return (eye[None, :, None, :, None] * w[:, :, :, None, :]).reshape(nl, nblk * bs, nblk * bs)


def kernel(x, positions, norm_ffn1, w_ffn1_in, w_ffn1_out, norm_mix, w_in, conv_w, conv_b,
           gate_a_w, gate_a_b, gate_x_w, gate_x_b, lru_lambda, w_out, norm_ffn2,
           w_ffn2_in, w_ffn2_out, norm_final):
    bsz, seq, d = x.shape
    assert d == D_MODEL
    cos_t, sin_t = _rope_tables(positions)
    consts = _retention_constants()
    gf = norm_final[None, :]
    w1_in, w1_out = w_ffn1_in.astype(BF16), w_ffn1_out.astype(BF16)
    w2_in, w2_out = w_ffn2_in.astype(BF16), w_ffn2_out.astype(BF16)
    win = _pad_lanes(w_in.astype(BF16))
    wout = _pad_lanes(w_out.astype(BF16))
    wg = _pad_lanes(jnp.concatenate([_block_diag(gate_a_w), _block_diag(gate_x_w)], axis=2).astype(BF16))
    gb = jnp.concatenate([gate_a_b, gate_x_b], axis=1)
    for l in range(DEPTH):
        x2 = _ffn(x.reshape(bsz * seq, d), norm_ffn1[l][None, :], w1_in, w1_out, gf,
                  layer=l, final_norm=False)
        x3 = _mixer(x2.reshape(bsz, seq, d), cos_t, sin_t, norm_mix[l][None, :], win, conv_w[l],
                    conv_b[l][None, :], wg, gb[l][None, :], lru_lambda[l][None, :], consts, wout, layer=l)
        x = _ffn(x3.reshape(bsz * seq, d), norm_ffn2[l][None, :], w2_in, w2_out, gf,
                 layer=l, final_norm=(l == DEPTH - 1)).reshape(bsz, seq, d)
    return x
```

```python
import functools

import jax
import jax.numpy as jnp
from jax import lax
from jax.experimental import pallas as pl
from jax.experimental.pallas import tpu as pltpu

F32 = jnp.float32
BF16 = jnp.bfloat16

D_MODEL = 1024
DEPTH = 2
D_RNN = 512
D_RET = 512
N_RNN_BLOCKS = 8
RNN_BLOCK = D_RNN // N_RNN_BLOCKS
CONV_WIDTH = 4
LRU_C = 8.0
N_RET_HEADS = 4
RET_HEAD_DIM = D_RET // N_RET_HEADS
CHUNK = 128
ROPE_BASE = 10000.0
D_FF = 2816
EPS = 1e-6
D_IN = 2 * D_RNN + 4 * D_RET

LANES = 128
SUBLANES = 8
MXU_DIM = 256
VMEM_LIMIT_BYTES = 60000 * 1024

FFN_ROWS = 1024
FFN_COLS = MXU_DIM
MIX_NB = 8
MIX_T = CHUNK
HALO = SUBLANES
MIX_PAIR = 2
MIX_UNROLL = 4
ROPE_T = 256


def _rmsnorm(x, g):
    return x * lax.rsqrt(jnp.mean(x * x, axis=-1, keepdims=True) + EPS) * g


def _resident(shape, layer=None):
    zeros = (0,) * len(shape)
    if layer is None:
        return pl.BlockSpec(shape, lambda *_: zeros, pipeline_mode=pl.Buffered(1))
    return pl.BlockSpec((None,) + tuple(shape), lambda *_: (layer,) + zeros, pipeline_mode=pl.Buffered(1))


def _ffn_kernel(x_ref, g_ref, w1_ref, w2_ref, gf_ref, o_ref, act_ref, *, final_norm):
    xb = _rmsnorm(x_ref[...], g_ref[...]).astype(BF16)
    for j in range(D_FF // FFN_COLS):
        lo = j * FFN_COLS
        gate = jnp.dot(xb, w1_ref[:, lo:lo + FFN_COLS], preferred_element_type=F32)
        up = jnp.dot(xb, w1_ref[:, D_FF + lo:D_FF + lo + FFN_COLS], preferred_element_type=F32)
        act_ref[:, lo:lo + FFN_COLS] = (gate * jax.nn.sigmoid(gate) * up).astype(BF16)
    y = jnp.dot(act_ref[...], w2_ref[...], preferred_element_type=F32)
    out = x_ref[...] + 0.5 * y
    if final_norm:
        out = _rmsnorm(out, gf_ref[...])
    o_ref[...] = out


def _ffn(x2d, g, w1, w2, gf, *, layer, final_norm):
    rows = x2d.shape[0]
    assert rows % FFN_ROWS == 0 and D_FF % FFN_COLS == 0
    return pl.pallas_call(
        functools.partial(_ffn_kernel, final_norm=final_norm),
        out_shape=jax.ShapeDtypeStruct(x2d.shape, F32),
        grid=(rows // FFN_ROWS,),
        in_specs=[
            pl.BlockSpec((FFN_ROWS, D_MODEL), lambda i: (i, 0)),
            _resident((1, D_MODEL)),
            _resident((D_MODEL, 2 * D_FF), layer),
            _resident((D_FF, D_MODEL), layer),
            _resident((1, D_MODEL)),
        ],
        out_specs=pl.BlockSpec((FFN_ROWS, D_MODEL), lambda i: (i, 0)),
        scratch_shapes=[pltpu.VMEM((FFN_ROWS, D_FF), BF16)],
        compiler_params=pltpu.CompilerParams(
            dimension_semantics=("parallel",), vmem_limit_bytes=VMEM_LIMIT_BYTES),
        name="ffn_final" if final_norm else "ffn",
    )(x2d, g, w1, w2, gf)


def _rope_kernel(pos_ref, freq_ref, sign_ref, cos_ref, sin_ref):
    pos = pos_ref[...].astype(F32)
    freq = freq_ref[...]
    sign = sign_ref[...]
    nseq = pos.shape[1]
    half = RET_HEAD_DIM // 2
    low = lax.broadcasted_iota(jnp.int32, (pos.shape[0], RET_HEAD_DIM), 1) < half
    for b in range(nseq // 2):
        ang = jnp.where(low, pos[:, b:b + 1], pos[:, b + nseq // 2:b + nseq // 2 + 1]) * freq
        c = jnp.cos(ang)
        s = jnp.sin(ang)
        c_sw = pltpu.roll(c, half, 1)
        s_sw = pltpu.roll(s, half, 1)
        cos_ref[b] = jnp.where(low, c, c_sw)
        cos_ref[b + nseq // 2] = jnp.where(low, c_sw, c)
        sin_ref[b] = jnp.where(low, s, s_sw) * sign
        sin_ref[b + nseq // 2] = jnp.where(low, s_sw, s) * sign


def _rope_tables(positions):
    bsz, seq = positions.shape
    half = RET_HEAD_DIM // 2
    inv_freq = ROPE_BASE ** (-jnp.arange(half, dtype=F32) / half)
    freq = jnp.concatenate([inv_freq, inv_freq])[None, :]
    sign = jnp.concatenate([-jnp.ones((half,), F32), jnp.ones((half,), F32)])[None, :]
    out = jax.ShapeDtypeStruct((bsz, seq, RET_HEAD_DIM), F32)
    return pl.pallas_call(
        _rope_kernel,
        out_shape=(out, out),
        grid=(seq // ROPE_T,),
        in_specs=[
            pl.BlockSpec((ROPE_T, bsz), lambda i: (i, 0)),
            _resident((1, RET_HEAD_DIM)),
            _resident((1, RET_HEAD_DIM)),
        ],
        out_specs=(
            pl.BlockSpec((bsz, ROPE_T, RET_HEAD_DIM), lambda i: (0, i, 0)),
            pl.BlockSpec((bsz, ROPE_T, RET_HEAD_DIM), lambda i: (0, i, 0)),
        ),
        compiler_params=pltpu.CompilerParams(
            dimension_semantics=("parallel",), vmem_limit_bytes=VMEM_LIMIT_BYTES),
        name="rope_tables",
    )(positions.T, freq, sign)


def _mixer_kernel(x_ref, cos_ref, sin_ref, gn_ref, win_ref, cw_ref, cb_ref, wg_ref, gb_ref, lam_ref,
                  dec_ref, xi_ref, zeta_ref, cd_ref, wout_ref, o_ref,
                  xb_ref, pg_ref, vb_ref, g_ref, xc_ref, a_ref, u_ref, lhs_ref, kz_ref,
                  y_ref, h_ref, kv_ref):
    nb, t = MIX_NB, MIX_T
    rows = nb * t
    n_slab = D_RNN // LANES

    @pl.when(pl.program_id(1) == 0)
    def _reset_state():
        xc_ref[:, 0:HALO, :] = jnp.zeros((nb, HALO, D_RNN), F32)
        h_ref[...] = jnp.zeros(h_ref.shape, F32)
        kv_ref[...] = jnp.zeros(kv_ref.shape, F32)

    xb_ref[...] = _rmsnorm(x_ref[...].reshape(rows, D_MODEL), gn_ref[...]).astype(BF16)

    for half in range(2):
        rs = slice(half * (rows // 2), (half + 1) * (rows // 2))
        bs = slice(half * (nb // 2), (half + 1) * (nb // 2))
        p = jnp.dot(xb_ref[rs, :], win_ref[:, 0:2 * D_RNN], preferred_element_type=F32)
        xc_ref[bs, HALO:HALO + t, :] = p[:, 0:D_RNN].reshape(nb // 2, t, D_RNN)
        pg_ref[rs, :] = p[:, D_RNN:2 * D_RNN]

    cw = cw_ref[...]
    cb = cb_ref[...]
    gb = gb_ref[...]
    lam = lam_ref[...]
    neg_c_sp = -LRU_C * (jnp.maximum(-lam, 0.0) + jnp.log1p(jnp.exp(-jnp.abs(lam))))

    def gates_and_ret_proj(j, carry):
        r0 = pl.multiple_of(j * (MIX_PAIR * t), MIX_PAIR * t)
        parts = []
        for s in range(MIX_PAIR):
            b = j * MIX_PAIR + s
            xcv_s = cb + cw[CONV_WIDTH - 1:CONV_WIDTH] * xc_ref[b, HALO:HALO + t, :]
            for k in range(CONV_WIDTH - 1):
                sh = CONV_WIDTH - 1 - k
                xcv_s = xcv_s + cw[k:k + 1] * xc_ref[b, HALO - sh:HALO - sh + t, :]
            parts.append(xcv_s)
        xcv = jnp.concatenate(parts, axis=0)
        pr = jnp.dot(xb_ref[pl.ds(r0, MIX_PAIR * t), :], win_ref[:, 2 * D_RNN:D_IN],
                     preferred_element_type=F32)
        vb_ref[pl.ds(r0, MIX_PAIR * t), :] = pr[:, 2 * D_RET:3 * D_RET].astype(BF16)
        g_ref[pl.ds(r0, MIX_PAIR * t), :] = pr[:, 3 * D_RET:4 * D_RET]
        for s in range(MIX_PAIR):
            b = j * MIX_PAIR + s
            cos = cos_ref[b]
            sin = sin_ref[b]
            for h in range(N_RET_HEADS):
                lo = h * RET_HEAD_DIM
                q = pr[s * t:(s + 1) * t, lo:lo + RET_HEAD_DIM]
                k = pr[s * t:(s + 1) * t, D_RET + lo:D_RET + lo + RET_HEAD_DIM]
                q = q * cos + pltpu.roll(q, RET_HEAD_DIM // 2, 1) * sin
                k = k * cos + pltpu.roll(k, RET_HEAD_DIM // 2, 1) * sin
                scores = lax.dot_general(q.astype(BF16), k.astype(BF16), (((1,), (1,)), ((), ())),
                                         preferred_element_type=F32) * dec_ref[h]
                lhs_ref[b, h, :, 0:CHUNK] = scores.astype(BF16)
                lhs_ref[b, h, :, CHUNK:CHUNK + RET_HEAD_DIM] = (q * xi_ref[h]).astype(BF16)
                kz_ref[b, h] = (k * zeta_ref[h]).astype(BF16)
        for c in range(n_slab):
            sl = slice(c * LANES, (c + 1) * LANES)
            xs = xcv[:, sl]
            gpre = jnp.dot(xs.astype(BF16), wg_ref[c, :, 0:2 * LANES], preferred_element_type=F32)
            r_gate = jax.nn.sigmoid(gpre[:, 0:LANES] + gb[:, sl])
            i_gate = jax.nn.sigmoid(
                gpre[:, LANES:2 * LANES] + gb[:, D_RNN + c * LANES:D_RNN + (c + 1) * LANES])
            log_a = r_gate * neg_c_sp[:, sl]
            a = jnp.exp(log_a)
            u = jnp.sqrt(jnp.tanh(-log_a) * (1.0 + a * a)) * (i_gate * xs)
            for s in range(MIX_PAIR):
                b = j * MIX_PAIR + s
                a_ref[c, pl.ds(b, t, stride=nb), :] = a[s * t:(s + 1) * t, :]
                u_ref[c, pl.ds(b, t, stride=nb), :] = u[s * t:(s + 1) * t, :]
        return carry

    lax.fori_loop(0, nb // MIX_PAIR, gates_and_ret_proj, 0, unroll=nb // MIX_PAIR)
    xc_ref[:, 0:HALO, :] = xc_ref[:, t:t + HALO, :]

    def scan_body(s, hs):
        new = []
        for c in range(n_slab):
            row = pl.multiple_of(s * nb, nb)
            h_c = a_ref[c, pl.ds(row, nb), :] * hs[c] + u_ref[c, pl.ds(row, nb), :]
            u_ref[c, pl.ds(row, nb), :] = h_c
            new.append(h_c)
        return tuple(new)

    hs = lax.fori_loop(0, t, scan_body, tuple(h_ref[c] for c in range(n_slab)), unroll=8)
    for c in range(n_slab):
        h_ref[c] = hs[c]

    def mix_out(b, carry):
        r0 = pl.multiple_of(b * t, t)
        for c in range(n_slab):
            h_c = u_ref[c, pl.ds(b, t, stride=nb), :]
            y_ref[pl.ds(r0, t), c * LANES:(c + 1) * LANES] = (
                h_c * jax.nn.gelu(pg_ref[pl.ds(r0, t), c * LANES:(c + 1) * LANES])).astype(BF16)
        for h in range(N_RET_HEADS):
            lo = h * RET_HEAD_DIM
            vb = vb_ref[pl.ds(r0, t), lo:lo + RET_HEAD_DIM]
            g = g_ref[pl.ds(r0, t), lo:lo + RET_HEAD_DIM]
            state = kv_ref[b, h]
            rhs = jnp.concatenate([vb, state.astype(BF16)], axis=0)
            ret = jnp.dot(lhs_ref[b, h], rhs, preferred_element_type=F32)
            kv_ref[b, h] = cd_ref[h] * state + lax.dot_general(
                kz_ref[b, h], vb, (((0,), (0,)), ((), ())), preferred_element_type=F32)
            ret = ret * lax.rsqrt(jnp.mean(ret * ret, axis=-1, keepdims=True) + EPS)
            y_ref[pl.ds(r0, t), D_RNN + lo:D_RNN + lo + RET_HEAD_DIM] = (
                ret * (g * jax.nn.sigmoid(g))).astype(BF16)
        acc = x_ref[b]
        for kt in range(D_MODEL // MXU_DIM):
            acc = acc + jnp.dot(y_ref[pl.ds(r0, t), kt * MXU_DIM:(kt + 1) * MXU_DIM],
                                wout_ref[kt * MXU_DIM:(kt + 1) * MXU_DIM, 0:D_MODEL],
                                preferred_element_type=F32)
        o_ref[b] = acc
        return carry

    lax.fori_loop(0, nb, mix_out, 0, unroll=MIX_UNROLL)


def _retention_constants():
    n_heads, c = N_RET_HEADS, CHUNK
    log_gamma = jnp.log1p(-(2.0 ** (-5.0 - jnp.arange(n_heads, dtype=F32))))
    idx = jnp.arange(c, dtype=F32)
    rel = idx[:, None] - idx[None, :]
    decay = jnp.where(rel >= 0, jnp.exp(jnp.maximum(rel, 0.0)[None] * log_gamma[:, None, None]), 0.0)
    xi = jnp.exp((idx + 1.0)[None, :] * log_gamma[:, None])
    zeta = jnp.exp((c - 1.0 - idx)[None, :] * log_gamma[:, None])
    chunk_decay = jnp.exp(c * log_gamma)
    scale = RET_HEAD_DIM ** -0.5
    dec = decay * scale
    xi_b = jnp.broadcast_to(xi[:, :, None], (n_heads, c, RET_HEAD_DIM))
    zeta_b = jnp.broadcast_to((zeta * scale)[:, :, None], (n_heads, c, RET_HEAD_DIM))
    cd_b = jnp.broadcast_to(chunk_decay[:, None, None], (n_heads, RET_HEAD_DIM, RET_HEAD_DIM))
    return dec, xi_b, zeta_b, cd_b


def _mixer(x, cos_t, sin_t, gn, win, cw, cb, wg, gb, lam, consts, wout, *, layer):
    bsz, seq, _ = x.shape
    nb, t = MIX_NB, MIX_T
    assert bsz % nb == 0 and seq % t == 0 and t == CHUNK
    dec, xi_b, zeta_b, cd_b = consts
    rows = nb * t
    n_slab = D_RNN // LANES
    tile = lambda i, j: (i, j, 0)
    hc = (N_RET_HEADS, CHUNK, RET_HEAD_DIM)
    return pl.pallas_call(
        _mixer_kernel,
        out_shape=jax.ShapeDtypeStruct(x.shape, F32),
        grid=(bsz // nb, seq // t),
        in_specs=[
            pl.BlockSpec((nb, t, D_MODEL), tile),
            pl.BlockSpec((nb, t, RET_HEAD_DIM), tile),
            pl.BlockSpec((nb, t, RET_HEAD_DIM), tile),
            _resident((1, D_MODEL)),
            _resident((D_MODEL, D_IN + LANES), layer),
            _resident((CONV_WIDTH, D_RNN)),
            _resident((1, D_RNN)),
            _resident((D_RNN // LANES, LANES, 3 * LANES), layer),
            _resident((1, 2 * D_RNN)),
            _resident((1, D_RNN)),
            _resident(hc),
            _resident(hc),
            _resident(hc),
            _resident(hc),
            _resident((D_MODEL, D_MODEL + LANES), layer),
        ],
        out_specs=pl.BlockSpec((nb, t, D_MODEL), tile),
        scratch_shapes=[
            pltpu.VMEM((rows, D_MODEL), BF16),
            pltpu.VMEM((rows, D_RNN), F32),
            pltpu.VMEM((rows, D_RET), BF16),
            pltpu.VMEM((rows, D_RET), F32),
            pltpu.VMEM((nb, HALO + t, D_RNN), F32),
            pltpu.VMEM((n_slab, t * nb, LANES), F32),
            pltpu.VMEM((n_slab, t * nb, LANES), F32),
            pltpu.VMEM((nb, N_RET_HEADS, t, CHUNK + RET_HEAD_DIM), BF16),
            pltpu.VMEM((nb, N_RET_HEADS, t, RET_HEAD_DIM), BF16),
            pltpu.VMEM((rows, D_MODEL), BF16),
            pltpu.VMEM((n_slab, nb, LANES), F32),
            pltpu.VMEM((nb, N_RET_HEADS, RET_HEAD_DIM, RET_HEAD_DIM), F32),
        ],
        compiler_params=pltpu.CompilerParams(
            dimension_semantics=("parallel", "arbitrary"), vmem_limit_bytes=VMEM_LIMIT_BYTES),
        name="mixer",
    )(x, cos_t, sin_t, gn, win, cw, cb, wg, gb, lam, dec, xi_b, zeta_b, cd_b, wout)


def _pad_lanes(w):
    return jnp.pad(w, [(0, 0)] * (w.ndim - 1) + [(0, LANES)])


def _slab_block_diag(w):
    nl, nblk, bs, _ = w.shape
    per = LANES // bs
    eye = jnp.eye(per, dtype=w.dtype)
    wb = w.reshape(nl, nblk // per, per, bs, bs)
    return (eye[None, None, :, None, :, None] * wb[:, :, :, :, None, :]).reshape(
        nl, nblk // per, LANES, LANES)


def kernel(x, positions, norm_ffn1, w_ffn1_in, w_ffn1_out, norm_mix, w_in, conv_w, conv_b,
           gate_a_w, gate_a_b, gate_x_w, gate_x_b, lru_lambda, w_out, norm_ffn2,
           w_ffn2_in, w_ffn2_out, norm_final):
    bsz, seq, d = x.shape
    assert d == D_MODEL
    cos_t, sin_t = _rope_tables(positions)
    consts = _retention_constants()
    gf = norm_final[None, :]
    w1_in, w1_out = w_ffn1_in.astype(BF16), w_ffn1_out.astype(BF16)
    w2_in, w2_out = w_ffn2_in.astype(BF16), w_ffn2_out.astype(BF16)
    win = _pad_lanes(w_in.astype(BF16))
    wout = _pad_lanes(w_out.astype(BF16))
    wg = _pad_lanes(jnp.concatenate(
        [_slab_block_diag(gate_a_w), _slab_block_diag(gate_x_w)], axis=3).astype(BF16))
    gb = jnp.concatenate([gate_a_b, gate_x_b], axis=1)
    for l in range(DEPTH):
        x2 = _ffn(x.reshape(bsz * seq, d), norm_ffn1[l][None, :], w1_in, w1_out, gf,
                  layer=l, final_norm=False)
        x3 = _mixer(x2.reshape(bsz, seq, d), cos_t, sin_t, norm_mix[l][None, :], win, conv_w[l],
                    conv_b[l][None, :], wg, gb[l][None, :], lru_lambda[l][None, :], consts, wout, layer=l)
        x = _ffn(x3.reshape(bsz * seq, d), norm_ffn2[l][None, :], w2_in, w2_out, gf,
                 layer=l, final_norm=(l == DEPTH - 1)).reshape(bsz, seq, d)
    return x
```

```python
import functools

import jax
import jax.numpy as jnp
from jax import lax
from jax.experimental import pallas as pl
from jax.experimental.pallas import tpu as pltpu

F32 = jnp.float32
BF16 = jnp.bfloat16

D_MODEL = 1024
DEPTH = 2
D_RNN = 512
D_RET = 512
N_RNN_BLOCKS = 8
RNN_BLOCK = D_RNN // N_RNN_BLOCKS
CONV_WIDTH = 4
LRU_C = 8.0
N_RET_HEADS = 4
RET_HEAD_DIM = D_RET // N_RET_HEADS
CHUNK = 128
ROPE_BASE = 10000.0
D_FF = 2816
EPS = 1e-6
D_IN = 2 * D_RNN + 4 * D_RET

LANES = 128
SUBLANES = 8
MXU_DIM = 256
VMEM_LIMIT_BYTES = 60000 * 1024

FFN_ROWS = 1024
FFN_COLS = MXU_DIM
MIX_NB = 8
MIX_T = CHUNK
HALO = SUBLANES
MIX_PAIR = 2
MIX_UNROLL = 4
ROPE_T = 256


def _rmsnorm(x, g):
    return x * lax.rsqrt(jnp.mean(x * x, axis=-1, keepdims=True) + EPS) * g


def _resident(shape, layer=None):
    zeros = (0,) * len(shape)
    if layer is None:
        return pl.BlockSpec(shape, lambda *_: zeros, pipeline_mode=pl.Buffered(1))
    return pl.BlockSpec((None,) + tuple(shape), lambda *_: (layer,) + zeros, pipeline_mode=pl.Buffered(1))


def _ffn_kernel(x_ref, g_ref, w1_ref, w2_ref, gf_ref, o_ref, act_ref, *, final_norm):
    xb = _rmsnorm(x_ref[...], g_ref[...]).astype(BF16)
    for j in range(D_FF // FFN_COLS):
        lo = j * FFN_COLS
        gate = jnp.dot(xb, w1_ref[:, lo:lo + FFN_COLS], preferred_element_type=F32)
        up = jnp.dot(xb, w1_ref[:, D_FF + lo:D_FF + lo + FFN_COLS], preferred_element_type=F32)
        act_ref[:, lo:lo + FFN_COLS] = (gate * jax.nn.sigmoid(gate) * up).astype(BF16)
    y = jnp.dot(act_ref[...], w2_ref[...], preferred_element_type=F32)
    out = x_ref[...] + 0.5 * y
    if final_norm:
        out = _rmsnorm(out, gf_ref[...])
    o_ref[...] = out


def _ffn(x2d, g, w1, w2, gf, *, layer, final_norm):
    rows = x2d.shape[0]
    assert rows % FFN_ROWS == 0 and D_FF % FFN_COLS == 0
    return pl.pallas_call(
        functools.partial(_ffn_kernel, final_norm=final_norm),
        out_shape=jax.ShapeDtypeStruct(x2d.shape, F32),
        grid=(rows // FFN_ROWS,),
        in_specs=[
            pl.BlockSpec((FFN_ROWS, D_MODEL), lambda i: (i, 0)),
            _resident((1, D_MODEL)),
            _resident((D_MODEL, 2 * D_FF), layer),
            _resident((D_FF, D_MODEL), layer),
            _resident((1, D_MODEL)),
        ],
        out_specs=pl.BlockSpec((FFN_ROWS, D_MODEL), lambda i: (i, 0)),
        scratch_shapes=[pltpu.VMEM((FFN_ROWS, D_FF), BF16)],
        compiler_params=pltpu.CompilerParams(
            dimension_semantics=("parallel",), vmem_limit_bytes=VMEM_LIMIT_BYTES),
        name="ffn_final" if final_norm else "ffn",
    )(x2d, g, w1, w2, gf)


def _rope_kernel(pos_ref, freq_ref, sign_ref, cos_ref, sin_ref):
    pos = pos_ref[...]
    freq = freq_ref[...]
    sign = sign_ref[...]
    nt, nseq = pos.shape
    half = RET_HEAD_DIM // 2
    step = lax.broadcasted_iota(jnp.int32, (nt, nseq), 0)
    consecutive = jnp.max(jnp.abs(pos - (pos[0:1, :] + step))) == 0

    @pl.when(consecutive)
    def _angle_addition():
        ang_t = lax.broadcasted_iota(jnp.int32, (nt, RET_HEAD_DIM), 0).astype(F32) * freq
        c_t = jnp.cos(ang_t)
        s_t = jnp.sin(ang_t)
        for b in range(nseq):
            ang_0 = pos[0:1, b:b + 1].astype(F32) * freq
            c_0 = jnp.cos(ang_0)
            s_0 = jnp.sin(ang_0)
            cos_ref[b] = c_0 * c_t - s_0 * s_t
            sin_ref[b] = (s_0 * c_t + c_0 * s_t) * sign

    @pl.when(jnp.logical_not(consecutive))
    def _direct():
        posf = pos.astype(F32)
        low = lax.broadcasted_iota(jnp.int32, (nt, RET_HEAD_DIM), 1) < half
        for b in range(nseq // 2):
            ang = jnp.where(low, posf[:, b:b + 1], posf[:, b + nseq // 2:b + nseq // 2 + 1]) * freq
            c = jnp.cos(ang)
            s = jnp.sin(ang)
            c_sw = pltpu.roll(c, half, 1)
            s_sw = pltpu.roll(s, half, 1)
            cos_ref[b] = jnp.where(low, c, c_sw)
            cos_ref[b + nseq // 2] = jnp.where(low, c_sw, c)
            sin_ref[b] = jnp.where(low, s, s_sw) * sign
            sin_ref[b + nseq // 2] = jnp.where(low, s_sw, s) * sign


def _rope_tables(positions):
    bsz, seq = positions.shape
    half = RET_HEAD_DIM // 2
    inv_freq = ROPE_BASE ** (-jnp.arange(half, dtype=F32) / half)
    freq = jnp.concatenate([inv_freq, inv_freq])[None, :]
    sign = jnp.concatenate([-jnp.ones((half,), F32), jnp.ones((half,), F32)])[None, :]
    out = jax.ShapeDtypeStruct((bsz, seq, RET_HEAD_DIM), F32)
    return pl.pallas_call(
        _rope_kernel,
        out_shape=(out, out),
        grid=(seq // ROPE_T,),
        in_specs=[
            pl.BlockSpec((ROPE_T, bsz), lambda i: (i, 0)),
            _resident((1, RET_HEAD_DIM)),
            _resident((1, RET_HEAD_DIM)),
        ],
        out_specs=(
            pl.BlockSpec((bsz, ROPE_T, RET_HEAD_DIM), lambda i: (0, i, 0)),
            pl.BlockSpec((bsz, ROPE_T, RET_HEAD_DIM), lambda i: (0, i, 0)),
        ),
        compiler_params=pltpu.CompilerParams(
            dimension_semantics=("parallel",), vmem_limit_bytes=VMEM_LIMIT_BYTES),
        name="rope_tables",
    )(positions.T, freq, sign)


def _mixer_kernel(x_ref, cos_ref, sin_ref, gn_ref, win_ref, cw_ref, cb_ref, wg_ref, gb_ref, lam_ref,
                  dec_ref, xi_ref, zeta_ref, cd_ref, wout_ref, o_ref,
                  xb_ref, pg_ref, vb_ref, g_ref, xc_ref, a_ref, u_ref, lhs_ref, kz_ref,
                  y_ref, h_ref, kv_ref):
    nb, t = MIX_NB, MIX_T
    rows = nb * t
    n_slab = D_RNN // LANES

    @pl.when(pl.program_id(1) == 0)
    def _reset_state():
        xc_ref[:, 0:HALO, :] = jnp.zeros((nb, HALO, D_RNN), F32)
        h_ref[...] = jnp.zeros(h_ref.shape, F32)
        kv_ref[...] = jnp.zeros(kv_ref.shape, F32)

    xb_ref[...] = _rmsnorm(x_ref[...].reshape(rows, D_MODEL), gn_ref[...]).astype(BF16)

    for half in range(2):
        rs = slice(half * (rows // 2), (half + 1) * (rows // 2))
        bs = slice(half * (nb // 2), (half + 1) * (nb // 2))
        p = jnp.dot(xb_ref[rs, :], win_ref[:, 0:2 * D_RNN], preferred_element_type=F32)
        xc_ref[bs, HALO:HALO + t, :] = p[:, 0:D_RNN].reshape(nb // 2, t, D_RNN)
        pg_ref[rs, :] = p[:, D_RNN:2 * D_RNN]

    cw = cw_ref[...]
    cb = cb_ref[...]
    gb = gb_ref[...]
    lam = lam_ref[...]
    neg_c_sp = -LRU_C * (jnp.maximum(-lam, 0.0) + jnp.log1p(jnp.exp(-jnp.abs(lam))))

    def gates_and_ret_proj(j, carry):
        r0 = pl.multiple_of(j * (MIX_PAIR * t), MIX_PAIR * t)
        parts = []
        for s in range(MIX_PAIR):
            b = j * MIX_PAIR + s
            xcv_s = cb + cw[CONV_WIDTH - 1:CONV_WIDTH] * xc_ref[b, HALO:HALO + t, :]
            for k in range(CONV_WIDTH - 1):
                sh = CONV_WIDTH - 1 - k
                xcv_s = xcv_s + cw[k:k + 1] * xc_ref[b, HALO - sh:HALO - sh + t, :]
            parts.append(xcv_s)
        xcv = jnp.concatenate(parts, axis=0)
        pr = jnp.dot(xb_ref[pl.ds(r0, MIX_PAIR * t), :], win_ref[:, 2 * D_RNN:D_IN],
                     preferred_element_type=F32)
        vb_ref[pl.ds(r0, MIX_PAIR * t), :] = pr[:, 2 * D_RET:3 * D_RET].astype(BF16)
        g_ref[pl.ds(r0, MIX_PAIR * t), :] = pr[:, 3 * D_RET:4 * D_RET]
        for s in range(MIX_PAIR):
            b = j * MIX_PAIR + s
            cos = cos_ref[b]
            sin = sin_ref[b]
            for h in range(N_RET_HEADS):
                lo = h * RET_HEAD_DIM
                q = pr[s * t:(s + 1) * t, lo:lo + RET_HEAD_DIM]
                k = pr[s * t:(s + 1) * t, D_RET + lo:D_RET + lo + RET_HEAD_DIM]
                q = q * cos + pltpu.roll(q, RET_HEAD_DIM // 2, 1) * sin
                k = k * cos + pltpu.roll(k, RET_HEAD_DIM // 2, 1) * sin
                scores = lax.dot_general(q.astype(BF16), k.astype(BF16), (((1,), (1,)), ((), ())),
                                         preferred_element_type=F32) * dec_ref[h]
                lhs_ref[b, h, :, 0:CHUNK] = scores.astype(BF16)
                lhs_ref[b, h, :, CHUNK:CHUNK + RET_HEAD_DIM] = (q * xi_ref[h]).astype(BF16)
                kz_ref[b, h] = (k * zeta_ref[h]).astype(BF16)
        for c in range(n_slab):
            sl = slice(c * LANES, (c + 1) * LANES)
            xs = xcv[:, sl]
            gpre = jnp.dot(xs.astype(BF16), wg_ref[c, :, 0:2 * LANES], preferred_element_type=F32)
            r_gate = jax.nn.sigmoid(gpre[:, 0:LANES] + gb[:, sl])
            i_gate = jax.nn.sigmoid(
                gpre[:, LANES:2 * LANES] + gb[:, D_RNN + c * LANES:D_RNN + (c + 1) * LANES])
            log_a = r_gate * neg_c_sp[:, sl]
            a = jnp.exp(log_a)
            u = jnp.sqrt(jnp.tanh(-log_a) * (1.0 + a * a)) * (i_gate * xs)
            for s in range(MIX_PAIR):
                b = j * MIX_PAIR + s
                a_ref[c, pl.ds(b, t, stride=nb), :] = a[s * t:(s + 1) * t, :]
                u_ref[c, pl.ds(b, t, stride=nb), :] = u[s * t:(s + 1) * t, :]
        return carry

    lax.fori_loop(0, nb // MIX_PAIR, gates_and_ret_proj, 0, unroll=nb // MIX_PAIR)
    xc_ref[:, 0:HALO, :] = xc_ref[:, t:t + HALO, :]

    def scan_body(s, hs):
        new = []
        for c in range(n_slab):
            row = pl.multiple_of(s * nb, nb)
            h_c = a_ref[c, pl.ds(row, nb), :] * hs[c] + u_ref[c, pl.ds(row, nb), :]
            u_ref[c, pl.ds(row, nb), :] = h_c
            new.append(h_c)
        return tuple(new)

    hs = lax.fori_loop(0, t, scan_body, tuple(h_ref[c] for c in range(n_slab)), unroll=8)
    for c in range(n_slab):
        h_ref[c] = hs[c]

    def mix_out(b, carry):
        r0 = pl.multiple_of(b * t, t)
        for c in range(n_slab):
            h_c = u_ref[c, pl.ds(b, t, stride=nb), :]
            y_ref[pl.ds(r0, t), c * LANES:(c + 1) * LANES] = (
                h_c * jax.nn.gelu(pg_ref[pl.ds(r0, t), c * LANES:(c + 1) * LANES])).astype(BF16)
        for h in range(N_RET_HEADS):
            lo = h * RET_HEAD_DIM
            vb = vb_ref[pl.ds(r0, t), lo:lo + RET_HEAD_DIM]
            g = g_ref[pl.ds(r0, t), lo:lo + RET_HEAD_DIM]
            state = kv_ref[b, h]
            rhs = jnp.concatenate([vb, state.astype(BF16)], axis=0)
            ret = jnp.dot(lhs_ref[b, h], rhs, preferred_element_type=F32)
            kv_ref[b, h] = cd_ref[h] * state + lax.dot_general(
                kz_ref[b, h], vb, (((0,), (0,)), ((), ())), preferred_element_type=F32)
            ret = ret * lax.rsqrt(jnp.mean(ret * ret, axis=-1, keepdims=True) + EPS)
            y_ref[pl.ds(r0, t), D_RNN + lo:D_RNN + lo + RET_HEAD_DIM] = (
                ret * (g * jax.nn.sigmoid(g))).astype(BF16)
        acc = x_ref[b]
        for kt in range(D_MODEL // MXU_DIM):
            acc = acc + jnp.dot(y_ref[pl.ds(r0, t), kt * MXU_DIM:(kt + 1) * MXU_DIM],
                                wout_ref[kt * MXU_DIM:(kt + 1) * MXU_DIM, 0:D_MODEL],
                                preferred_element_type=F32)
        o_ref[b] = acc
        return carry

    lax.fori_loop(0, nb, mix_out, 0, unroll=MIX_UNROLL)


def _retention_constants():
    n_heads, c = N_RET_HEADS, CHUNK
    log_gamma = jnp.log1p(-(2.0 ** (-5.0 - jnp.arange(n_heads, dtype=F32))))
    idx = jnp.arange(c, dtype=F32)
    rel = idx[:, None] - idx[None, :]
    decay = jnp.where(rel >= 0, jnp.exp(jnp.maximum(rel, 0.0)[None] * log_gamma[:, None, None]), 0.0)
    xi = jnp.exp((idx + 1.0)[None, :] * log_gamma[:, None])
    zeta = jnp.exp((c - 1.0 - idx)[None, :] * log_gamma[:, None])
    chunk_decay = jnp.exp(c * log_gamma)
    scale = RET_HEAD_DIM ** -0.5
    dec = decay * scale
    xi_b = jnp.broadcast_to(xi[:, :, None], (n_heads, c, RET_HEAD_DIM))
    zeta_b = jnp.broadcast_to((zeta * scale)[:, :, None], (n_heads, c, RET_HEAD_DIM))
    cd_b = jnp.broadcast_to(chunk_decay[:, None, None], (n_heads, RET_HEAD_DIM, RET_HEAD_DIM))
    return dec, xi_b, zeta_b, cd_b


def _mixer(x, cos_t, sin_t, gn, win, cw, cb, wg, gb, lam, consts, wout, *, layer):
    bsz, seq, _ = x.shape
    nb, t = MIX_NB, MIX_T
    assert bsz % nb == 0 and seq % t == 0 and t == CHUNK
    dec, xi_b, zeta_b, cd_b = consts
    rows = nb * t
    n_slab = D_RNN // LANES
    tile = lambda i, j: (i, j, 0)
    hc = (N_RET_HEADS, CHUNK, RET_HEAD_DIM)
    return pl.pallas_call(
        _mixer_kernel,
        out_shape=jax.ShapeDtypeStruct(x.shape, F32),
        grid=(bsz // nb, seq // t),
        in_specs=[
            pl.BlockSpec((nb, t, D_MODEL), tile),
            pl.BlockSpec((nb, t, RET_HEAD_DIM), tile),
            pl.BlockSpec((nb, t, RET_HEAD_DIM), tile),
            _resident((1, D_MODEL)),
            _resident((D_MODEL, D_IN + LANES), layer),
            _resident((CONV_WIDTH, D_RNN)),
            _resident((1, D_RNN)),
            _resident((D_RNN // LANES, LANES, 3 * LANES), layer),
            _resident((1, 2 * D_RNN)),
            _resident((1, D_RNN)),
            _resident(hc),
            _resident(hc),
            _resident(hc),
            _resident(hc),
            _resident((D_MODEL, D_MODEL + LANES), layer),
        ],
        out_specs=pl.BlockSpec((nb, t, D_MODEL), tile),
        scratch_shapes=[
            pltpu.VMEM((rows, D_MODEL), BF16),
            pltpu.VMEM((rows, D_RNN), F32),
            pltpu.VMEM((rows, D_RET), BF16),
            pltpu.VMEM((rows, D_RET), F32),
            pltpu.VMEM((nb, HALO + t, D_RNN), F32),
            pltpu.VMEM((n_slab, t * nb, LANES), F32),
            pltpu.VMEM((n_slab, t * nb, LANES), F32),
            pltpu.VMEM((nb, N_RET_HEADS, t, CHUNK + RET_HEAD_DIM), BF16),
            pltpu.VMEM((nb, N_RET_HEADS, t, RET_HEAD_DIM), BF16),
            pltpu.VMEM((rows, D_MODEL), BF16),
            pltpu.VMEM((n_slab, nb, LANES), F32),
            pltpu.VMEM((nb, N_RET_HEADS, RET_HEAD_DIM, RET_HEAD_DIM), F32),
        ],
        compiler_params=pltpu.CompilerParams(
            dimension_semantics=("parallel", "arbitrary"), vmem_limit_bytes=VMEM_LIMIT_BYTES),
        name="mixer",
    )(x, cos_t, sin_t, gn, win, cw, cb, wg, gb, lam, dec, xi_b, zeta_b, cd_b, wout)


def _pad_lanes(w):
    return jnp.pad(w, [(0, 0)] * (w.ndim - 1) + [(0, LANES)])


def _slab_block_diag(w):
    nl, nblk, bs, _ = w.shape
    per = LANES // bs
    eye = jnp.eye(per, dtype=w.dtype)
    wb = w.reshape(nl, nblk // per, per, bs, bs)
    return (eye[None, None, :, None, :, None] * wb[:, :, :, :, None, :]).reshape(
        nl, nblk // per, LANES, LANES)


def kernel(x, positions, norm_ffn1, w_ffn1_in, w_ffn1_out, norm_mix, w_in, conv_w, conv_b,
           gate_a_w, gate_a_b, gate_x_w, gate_x_b, lru_lambda, w_out, norm_ffn2,
           w_ffn2_in, w_ffn2_out, norm_final):
    bsz, seq, d = x.shape
    assert d == D_MODEL
    cos_t, sin_t = _rope_tables(positions)
    consts = _retention_constants()
    gf = norm_final[None, :]
    w1_in, w1_out = w_ffn1_in.astype(BF16), w_ffn1_out.astype(BF16)
    w2_in, w2_out = w_ffn2_in.astype(BF16), w_ffn2_out.astype(BF16)
    win = _pad_lanes(w_in.astype(BF16))
    wout = _pad_lanes(w_out.astype(BF16))
    wg = _pad_lanes(jnp.concatenate(
        [_slab_block_diag(gate_a_w), _slab_block_diag(gate_x_w)], axis=3).astype(BF16))
    gb = jnp.concatenate([gate_a_b, gate_x_b], axis=1)
    for l in range(DEPTH):
        x2 = _ffn(x.reshape(bsz * seq, d), norm_ffn1[l][None, :], w1_in, w1_out, gf,
                  layer=l, final_norm=False)
        x3 = _mixer(x2.reshape(bsz, seq, d), cos_t, sin_t, norm_mix[l][None, :], win, conv_w[l],
                    conv_b[l][None, :], wg, gb[l][None, :], lru_lambda[l][None, :], consts, wout, layer=l)
        x = _ffn(x3.reshape(bsz * seq, d), norm_ffn2[l][None, :], w2_in, w2_out, gf,
                 layer=l, final_norm=(l == DEPTH - 1)).reshape(bsz, seq, d)
    return x
```

```python
import functools

import jax
import jax.numpy as jnp
from jax import lax
from jax.experimental import pallas as pl
from jax.experimental.pallas import tpu as pltpu

F32 = jnp.float32
BF16 = jnp.bfloat16

D_MODEL = 1024
DEPTH = 2
D_RNN = 512
D_RET = 512
N_RNN_BLOCKS = 8
RNN_BLOCK = D_RNN // N_RNN_BLOCKS
CONV_WIDTH = 4
LRU_C = 8.0
N_RET_HEADS = 4
RET_HEAD_DIM = D_RET // N_RET_HEADS
CHUNK = 128
ROPE_BASE = 10000.0
D_FF = 2816
EPS = 1e-6
D_IN = 2 * D_RNN + 4 * D_RET

LANES = 128
SUBLANES = 8
MXU_DIM = 256
VMEM_LIMIT_BYTES = 60000 * 1024

FFN_ROWS = 1024
FFN_COLS = MXU_DIM
MIX_NB = 8
MIX_T = CHUNK
HALO = SUBLANES
MIX_PAIR = 2
MIX_UNROLL = 4
ROPE_T = 256
ROPE_FAST_MAX_POS = 1 << 16


def _rmsnorm(x, g):
    return x * lax.rsqrt(jnp.mean(x * x, axis=-1, keepdims=True) + EPS) * g


def _resident(shape, layer=None):
    zeros = (0,) * len(shape)
    if layer is None:
        return pl.BlockSpec(shape, lambda *_: zeros, pipeline_mode=pl.Buffered(1))
    return pl.BlockSpec((None,) + tuple(shape), lambda *_: (layer,) + zeros, pipeline_mode=pl.Buffered(1))


def _ffn_kernel(x_ref, g_ref, w1_ref, w2_ref, gf_ref, o_ref, act_ref, *, final_norm):
    xb = _rmsnorm(x_ref[...], g_ref[...]).astype(BF16)
    for j in range(D_FF // FFN_COLS):
        lo = j * FFN_COLS
        gate = jnp.dot(xb, w1_ref[:, lo:lo + FFN_COLS], preferred_element_type=F32)
        up = jnp.dot(xb, w1_ref[:, D_FF + lo:D_FF + lo + FFN_COLS], preferred_element_type=F32)
        act_ref[:, lo:lo + FFN_COLS] = (gate * jax.nn.sigmoid(gate) * up).astype(BF16)
    for hf in range(2):
        rs = slice(hf * (FFN_ROWS // 2), (hf + 1) * (FFN_ROWS // 2))
        y = jnp.dot(act_ref[rs, :], w2_ref[...], preferred_element_type=F32)
        out = x_ref[rs, :] + 0.5 * y
        if final_norm:
            out = _rmsnorm(out, gf_ref[...])
        o_ref[rs, :] = out


def _ffn(x2d, g, w1, w2, gf, *, layer, final_norm):
    rows = x2d.shape[0]
    assert rows % FFN_ROWS == 0 and D_FF % FFN_COLS == 0
    return pl.pallas_call(
        functools.partial(_ffn_kernel, final_norm=final_norm),
        out_shape=jax.ShapeDtypeStruct(x2d.shape, F32),
        grid=(rows // FFN_ROWS,),
        in_specs=[
            pl.BlockSpec((FFN_ROWS, D_MODEL), lambda i: (i, 0)),
            _resident((1, D_MODEL)),
            _resident((D_MODEL, 2 * D_FF), layer),
            _resident((D_FF, D_MODEL), layer),
            _resident((1, D_MODEL)),
        ],
        out_specs=pl.BlockSpec((FFN_ROWS, D_MODEL), lambda i: (i, 0)),
        scratch_shapes=[pltpu.VMEM((FFN_ROWS, D_FF), BF16)],
        compiler_params=pltpu.CompilerParams(
            dimension_semantics=("parallel",), vmem_limit_bytes=VMEM_LIMIT_BYTES),
        name="ffn_final" if final_norm else "ffn",
    )(x2d, g, w1, w2, gf)


def _rope_kernel(pos_ref, freq_ref, sign_ref, cos_ref, sin_ref):
    pos = pos_ref[...]
    freq = freq_ref[...]
    sign = sign_ref[...]
    nt, nseq = pos.shape
    half = RET_HEAD_DIM // 2
    step = lax.broadcasted_iota(jnp.int32, (nt, nseq), 0)
    p0 = pos[0:1, :]
    bad = (jnp.where(pos != p0 + step, 1, 0)
           + jnp.where(p0 < -ROPE_FAST_MAX_POS, 1, 0) + jnp.where(p0 > ROPE_FAST_MAX_POS - nt, 1, 0))
    consecutive = jnp.max(bad) == 0

    @pl.when(consecutive)
    def _angle_addition():
        ang_t = lax.broadcasted_iota(jnp.int32, (nt, RET_HEAD_DIM), 0).astype(F32) * freq
        c_t = jnp.cos(ang_t)
        s_t = jnp.sin(ang_t)
        for b in range(nseq):
            ang_0 = pos[0:1, b:b + 1].astype(F32) * freq
            c_0 = jnp.cos(ang_0)
            s_0 = jnp.sin(ang_0)
            cos_ref[b] = c_0 * c_t - s_0 * s_t
            sin_ref[b] = (s_0 * c_t + c_0 * s_t) * sign

    @pl.when(jnp.logical_not(consecutive))
    def _direct():
        posf = pos.astype(F32)
        low = lax.broadcasted_iota(jnp.int32, (nt, RET_HEAD_DIM), 1) < half
        for b in range(nseq // 2):
            ang = jnp.where(low, posf[:, b:b + 1], posf[:, b + nseq // 2:b + nseq // 2 + 1]) * freq
            c = jnp.cos(ang)
            s = jnp.sin(ang)
            c_sw = pltpu.roll(c, half, 1)
            s_sw = pltpu.roll(s, half, 1)
            cos_ref[b] = jnp.where(low, c, c_sw)
            cos_ref[b + nseq // 2] = jnp.where(low, c_sw, c)
            sin_ref[b] = jnp.where(low, s, s_sw) * sign
            sin_ref[b + nseq // 2] = jnp.where(low, s_sw, s) * sign


def _rope_tables(positions):
    bsz, seq = positions.shape
    half = RET_HEAD_DIM // 2
    inv_freq = ROPE_BASE ** (-jnp.arange(half, dtype=F32) / half)
    freq = jnp.concatenate([inv_freq, inv_freq])[None, :]
    sign = jnp.concatenate([-jnp.ones((half,), F32), jnp.ones((half,), F32)])[None, :]
    out = jax.ShapeDtypeStruct((bsz, seq, RET_HEAD_DIM), F32)
    return pl.pallas_call(
        _rope_kernel,
        out_shape=(out, out),
        grid=(seq // ROPE_T,),
        in_specs=[
            pl.BlockSpec((ROPE_T, bsz), lambda i: (i, 0)),
            _resident((1, RET_HEAD_DIM)),
            _resident((1, RET_HEAD_DIM)),
        ],
        out_specs=(
            pl.BlockSpec((bsz, ROPE_T, RET_HEAD_DIM), lambda i: (0, i, 0)),
            pl.BlockSpec((bsz, ROPE_T, RET_HEAD_DIM), lambda i: (0, i, 0)),
        ),
        compiler_params=pltpu.CompilerParams(
            dimension_semantics=("parallel",), vmem_limit_bytes=VMEM_LIMIT_BYTES),
        name="rope_tables",
    )(positions.T, freq, sign)


def _mixer_kernel(x_ref, cos_ref, sin_ref, gn_ref, win_ref, cw_ref, cb_ref, wg_ref, gb_ref, lam_ref,
                  dec_ref, xi_ref, zeta_ref, cd_ref, wout_ref, o_ref,
                  xb_ref, pg_ref, vb_ref, g_ref, xc_ref, a_ref, u_ref, lhs_ref, kz_ref,
                  y_ref, h_ref, kv_ref):
    nb, t = MIX_NB, MIX_T
    rows = nb * t
    n_slab = D_RNN // LANES

    @pl.when(pl.program_id(1) == 0)
    def _reset_state():
        xc_ref[:, 0:HALO, :] = jnp.zeros((nb, HALO, D_RNN), F32)
        h_ref[...] = jnp.zeros(h_ref.shape, F32)
        kv_ref[...] = jnp.zeros(kv_ref.shape, F32)

    xb_ref[...] = _rmsnorm(x_ref[...].reshape(rows, D_MODEL), gn_ref[...]).astype(BF16)

    for half in range(2):
        rs = slice(half * (rows // 2), (half + 1) * (rows // 2))
        bs = slice(half * (nb // 2), (half + 1) * (nb // 2))
        p = jnp.dot(xb_ref[rs, :], win_ref[:, 0:2 * D_RNN], preferred_element_type=F32)
        xc_ref[bs, HALO:HALO + t, :] = p[:, 0:D_RNN].reshape(nb // 2, t, D_RNN)
        pg_ref[rs, :] = p[:, D_RNN:2 * D_RNN]

    cw = cw_ref[...]
    cb = cb_ref[...]
    gb = gb_ref[...]
    lam = lam_ref[...]
    neg_c_sp = -LRU_C * (jnp.maximum(-lam, 0.0) + jnp.log1p(jnp.exp(-jnp.abs(lam))))

    def gates_and_ret_proj(j, carry):
        r0 = pl.multiple_of(j * (MIX_PAIR * t), MIX_PAIR * t)
        parts = []
        for s in range(MIX_PAIR):
            b = j * MIX_PAIR + s
            xcv_s = cb + cw[CONV_WIDTH - 1:CONV_WIDTH] * xc_ref[b, HALO:HALO + t, :]
            for k in range(CONV_WIDTH - 1):
                sh = CONV_WIDTH - 1 - k
                xcv_s = xcv_s + cw[k:k + 1] * xc_ref[b, HALO - sh:HALO - sh + t, :]
            parts.append(xcv_s)
        xcv = jnp.concatenate(parts, axis=0)
        pr = jnp.dot(xb_ref[pl.ds(r0, MIX_PAIR * t), :], win_ref[:, 2 * D_RNN:D_IN],
                     preferred_element_type=F32)
        vb_ref[pl.ds(r0, MIX_PAIR * t), :] = pr[:, 2 * D_RET:3 * D_RET].astype(BF16)
        g_ref[pl.ds(r0, MIX_PAIR * t), :] = pr[:, 3 * D_RET:4 * D_RET]
        for s in range(MIX_PAIR):
            b = j * MIX_PAIR + s
            cos = cos_ref[b]
            sin = sin_ref[b]
            for h in range(N_RET_HEADS):
                lo = h * RET_HEAD_DIM
                q = pr[s * t:(s + 1) * t, lo:lo + RET_HEAD_DIM]
                k = pr[s * t:(s + 1) * t, D_RET + lo:D_RET + lo + RET_HEAD_DIM]
                q = q * cos + pltpu.roll(q, RET_HEAD_DIM // 2, 1) * sin
                k = k * cos + pltpu.roll(k, RET_HEAD_DIM // 2, 1) * sin
                scores = lax.dot_general(q.astype(BF16), k.astype(BF16), (((1,), (1,)), ((), ())),
                                         preferred_element_type=F32) * dec_ref[h]
                lhs_ref[b, h, :, 0:CHUNK] = scores.astype(BF16)
                lhs_ref[b, h, :, CHUNK:CHUNK + RET_HEAD_DIM] = (q * xi_ref[h]).astype(BF16)
                kz_ref[b, h] = (k * zeta_ref[h]).astype(BF16)
        for c in range(n_slab):
            sl = slice(c * LANES, (c + 1) * LANES)
            xs = xcv[:, sl]
            gpre = jnp.dot(xs.astype(BF16), wg_ref[c, :, 0:2 * LANES], preferred_element_type=F32)
            r_gate = jax.nn.sigmoid(gpre[:, 0:LANES] + gb[:, sl])
            i_gate = jax.nn.sigmoid(
                gpre[:, LANES:2 * LANES] + gb[:, D_RNN + c * LANES:D_RNN + (c + 1) * LANES])
            log_a = r_gate * neg_c_sp[:, sl]
            a = jnp.exp(log_a)
            u = jnp.sqrt(jnp.tanh(-log_a) * (1.0 + a * a)) * (i_gate * xs)
            for s in range(MIX_PAIR):
                b = j * MIX_PAIR + s
                a_ref[c, pl.ds(b, t, stride=nb), :] = a[s * t:(s + 1) * t, :]
                u_ref[c, pl.ds(b, t, stride=nb), :] = u[s * t:(s + 1) * t, :]
        return carry

    lax.fori_loop(0, nb // MIX_PAIR, gates_and_ret_proj, 0, unroll=nb // MIX_PAIR)
    xc_ref[:, 0:HALO, :] = xc_ref[:, t:t + HALO, :]

    def scan_body(s, hs):
        new = []
        for c in range(n_slab):
            row = pl.multiple_of(s * nb, nb)
            h_c = a_ref[c, pl.ds(row, nb), :] * hs[c] + u_ref[c, pl.ds(row, nb), :]
            u_ref[c, pl.ds(row, nb), :] = h_c
            new.append(h_c)
        return tuple(new)

    hs = lax.fori_loop(0, t, scan_body, tuple(h_ref[c] for c in range(n_slab)), unroll=8)
    for c in range(n_slab):
        h_ref[c] = hs[c]

    def mix_out(b, carry):
        r0 = pl.multiple_of(b * t, t)
        for c in range(n_slab):
            h_c = u_ref[c, pl.ds(b, t, stride=nb), :]
            y_ref[pl.ds(r0, t), c * LANES:(c + 1) * LANES] = (
                h_c * jax.nn.gelu(pg_ref[pl.ds(r0, t), c * LANES:(c + 1) * LANES])).astype(BF16)
        for h in range(N_RET_HEADS):
            lo = h * RET_HEAD_DIM
            vb = vb_ref[pl.ds(r0, t), lo:lo + RET_HEAD_DIM]
            g = g_ref[pl.ds(r0, t), lo:lo + RET_HEAD_DIM]
            state = kv_ref[b, h]
            rhs = jnp.concatenate([vb, state.astype(BF16)], axis=0)
            ret = jnp.dot(lhs_ref[b, h], rhs, preferred_element_type=F32)
            kv_ref[b, h] = cd_ref[h] * state + lax.dot_general(
                kz_ref[b, h], vb, (((0,), (0,)), ((), ())), preferred_element_type=F32)
            ret = ret * lax.rsqrt(jnp.mean(ret * ret, axis=-1, keepdims=True) + EPS)
            y_ref[pl.ds(r0, t), D_RNN + lo:D_RNN + lo + RET_HEAD_DIM] = (
                ret * (g * jax.nn.sigmoid(g))).astype(BF16)
        acc = x_ref[b]
        for kt in range(D_MODEL // MXU_DIM):
            acc = acc + jnp.dot(y_ref[pl.ds(r0, t), kt * MXU_DIM:(kt + 1) * MXU_DIM],
                                wout_ref[kt * MXU_DIM:(kt + 1) * MXU_DIM, 0:D_MODEL],
                                preferred_element_type=F32)
        o_ref[b] = acc
        return carry

    lax.fori_loop(0, nb, mix_out, 0, unroll=MIX_UNROLL)


def _retention_constants():
    n_heads, c = N_RET_HEADS, CHUNK
    log_gamma = jnp.log1p(-(2.0 ** (-5.0 - jnp.arange(n_heads, dtype=F32))))
    idx = jnp.arange(c, dtype=F32)
    rel = idx[:, None] - idx[None, :]
    decay = jnp.where(rel >= 0, jnp.exp(jnp.maximum(rel, 0.0)[None] * log_gamma[:, None, None]), 0.0)
    xi = jnp.exp((idx + 1.0)[None, :] * log_gamma[:, None])
    zeta = jnp.exp((c - 1.0 - idx)[None, :] * log_gamma[:, None])
    chunk_decay = jnp.exp(c * log_gamma)
    scale = RET_HEAD_DIM ** -0.5
    dec = decay * scale
    xi_b = jnp.broadcast_to(xi[:, :, None], (n_heads, c, RET_HEAD_DIM))
    zeta_b = jnp.broadcast_to((zeta * scale)[:, :, None], (n_heads, c, RET_HEAD_DIM))
    cd_b = jnp.broadcast_to(chunk_decay[:, None, None], (n_heads, RET_HEAD_DIM, RET_HEAD_DIM))
    return dec, xi_b, zeta_b, cd_b


def _mixer(x, cos_t, sin_t, gn, win, cw, cb, wg, gb, lam, consts, wout, *, layer):
    bsz, seq, _ = x.shape
    nb, t = MIX_NB, MIX_T
    assert bsz % nb == 0 and seq % t == 0 and t == CHUNK
    dec, xi_b, zeta_b, cd_b = consts
    rows = nb * t
    n_slab = D_RNN // LANES
    tile = lambda i, j: (i, j, 0)
    hc = (N_RET_HEADS, CHUNK, RET_HEAD_DIM)
    return pl.pallas_call(
        _mixer_kernel,
        out_shape=jax.ShapeDtypeStruct(x.shape, F32),
        grid=(bsz // nb, seq // t),
        in_specs=[
            pl.BlockSpec((nb, t, D_MODEL), tile),
            pl.BlockSpec((nb, t, RET_HEAD_DIM), tile),
            pl.BlockSpec((nb, t, RET_HEAD_DIM), tile),
            _resident((1, D_MODEL)),
            _resident((D_MODEL, D_IN + LANES), layer),
            _resident((CONV_WIDTH, D_RNN)),
            _resident((1, D_RNN)),
            _resident((D_RNN // LANES, LANES, 3 * LANES), layer),
            _resident((1, 2 * D_RNN)),
            _resident((1, D_RNN)),
            _resident(hc),
            _resident(hc),
            _resident(hc),
            _resident(hc),
            _resident((D_MODEL, D_MODEL + LANES), layer),
        ],
        out_specs=pl.BlockSpec((nb, t, D_MODEL), tile),
        scratch_shapes=[
            pltpu.VMEM((rows, D_MODEL), BF16),
            pltpu.VMEM((rows, D_RNN), F32),
            pltpu.VMEM((rows, D_RET), BF16),
            pltpu.VMEM((rows, D_RET), F32),
            pltpu.VMEM((nb, HALO + t, D_RNN), F32),
            pltpu.VMEM((n_slab, t * nb, LANES), F32),
            pltpu.VMEM((n_slab, t * nb, LANES), F32),
            pltpu.VMEM((nb, N_RET_HEADS, t, CHUNK + RET_HEAD_DIM), BF16),
            pltpu.VMEM((nb, N_RET_HEADS, t, RET_HEAD_DIM), BF16),
            pltpu.VMEM((rows, D_MODEL), BF16),
            pltpu.VMEM((n_slab, nb, LANES), F32),
            pltpu.VMEM((nb, N_RET_HEADS, RET_HEAD_DIM, RET_HEAD_DIM), F32),
        ],
        compiler_params=pltpu.CompilerParams(
            dimension_semantics=("parallel", "arbitrary"), vmem_limit_bytes=VMEM_LIMIT_BYTES),
        name="mixer",
    )(x, cos_t, sin_t, gn, win, cw, cb, wg, gb, lam, dec, xi_b, zeta_b, cd_b, wout)


def _pad_lanes(w):
    return jnp.pad(w, [(0, 0)] * (w.ndim - 1) + [(0, LANES)])


def _slab_block_diag(w):
    nl, nblk, bs, _ = w.shape
    per = LANES // bs
    eye = jnp.eye(per, dtype=w.dtype)
    wb = w.reshape(nl, nblk // per, per, bs, bs)
    return (eye[None, None, :, None, :, None] * wb[:, :, :, :, None, :]).reshape(
        nl, nblk // per, LANES, LANES)


def kernel(x, positions, norm_ffn1, w_ffn1_in, w_ffn1_out, norm_mix, w_in, conv_w, conv_b,
           gate_a_w, gate_a_b, gate_x_w, gate_x_b, lru_lambda, w_out, norm_ffn2,
           w_ffn2_in, w_ffn2_out, norm_final):
    bsz, seq, d = x.shape
    assert d == D_MODEL
    cos_t, sin_t = _rope_tables(positions)
    consts = _retention_constants()
    gf = norm_final[None, :]
    w1_in, w1_out = w_ffn1_in.astype(BF16), w_ffn1_out.astype(BF16)
    w2_in, w2_out = w_ffn2_in.astype(BF16), w_ffn2_out.astype(BF16)
    win = _pad_lanes(w_in.astype(BF16))
    wout = _pad_lanes(w_out.astype(BF16))
    wg = _pad_lanes(jnp.concatenate(
        [_slab_block_diag(gate_a_w), _slab_block_diag(gate_x_w)], axis=3).astype(BF16))
    gb = jnp.concatenate([gate_a_b, gate_x_b], axis=1)
    for l in range(DEPTH):
        x2 = _ffn(x.reshape(bsz * seq, d), norm_ffn1[l][None, :], w1_in, w1_out, gf,
                  layer=l, final_norm=False)
        x3 = _mixer(x2.reshape(bsz, seq, d), cos_t, sin_t, norm_mix[l][None, :], win, conv_w[l],
                    conv_b[l][None, :], wg, gb[l][None, :], lru_lambda[l][None, :], consts, wout, layer=l)
        x = _ffn(x3.reshape(bsz * seq, d), norm_ffn2[l][None, :], w2_in, w2_out, gf,
                 layer=l, final_norm=(l == DEPTH - 1)).reshape(bsz, seq, d)
    return x
```
